```python
import jax, jax.numpy as jnp
from jax import lax
import numpy as np

D_MODEL = 2048
BATCH = 4
SEQ = 4096
DEPTH = 1
DEC_BATCH = 1
DEC_SEQ = 16384
PAST_LEN = 128

POOL_WINDOWS = (2, 4, 8, 16)
POOL_GROUPS = len(POOL_WINDOWS)
POOL_WIDTH = D_MODEL // 2
POOL_GROUP_DIM = POOL_WIDTH // POOL_GROUPS
HEAD_DIM = 128
ATTN_GROUPS = ((128, 1), (512, 4), (2048, 16))
HEADS_PER_GROUP = 4
N_HEADS = HEADS_PER_GROUP * len(ATTN_GROUPS)
ATTN_WIDTH = N_HEADS * HEAD_DIM
ATTN_OUT_WIDTH = HEADS_PER_GROUP * HEAD_DIM
ROPE_THETA = 10000.0
N_BRANCHES = 2
IN_WIDTH = POOL_WIDTH + 3 * ATTN_WIDTH + N_BRANCHES * D_MODEL
D_FF = ((8 * D_MODEL // 3 + 255) // 256) * 256
CONV_WIDTH = 3
EPS = 1e-6
NEG_INF = -1e30

kernel_name = 'hybrid_pool_dilated_attn_encoder'


def rms_norm(x, g):
    xf = x.astype(jnp.float32)
    inv = lax.rsqrt(jnp.mean(xf * xf, axis=-1, keepdims=True) + EPS)
    return (xf * inv * g.astype(jnp.float32)).astype(x.dtype)


def rope(x):
    S = x.shape[1]
    half = HEAD_DIM // 2
    inv_freq = ROPE_THETA ** (-jnp.arange(half, dtype=jnp.float32) / half)
    ang = jnp.arange(S, dtype=jnp.float32)[:, None] * inv_freq[None, :]
    bshape = (1, S) + (1,) * (x.ndim - 3) + (half,)
    cos = jnp.cos(ang).reshape(bshape)
    sin = jnp.sin(ang).reshape(bshape)
    xf = x.astype(jnp.float32)
    x1, x2 = xf[..., :half], xf[..., half:]
    return jnp.concatenate([x1 * cos - x2 * sin, x2 * cos + x1 * sin], axis=-1).astype(x.dtype)


def pool_mixer(u, w_pool, pool_scale):
    B, S, _ = u.shape
    ug = u.reshape(B, S, POOL_GROUPS, POOL_GROUP_DIM).astype(jnp.float32)
    csum = jnp.concatenate([jnp.zeros((B, 1, POOL_GROUPS, POOL_GROUP_DIM), jnp.float32),
                            jnp.cumsum(ug, axis=1)], axis=1)
    pos = jnp.arange(S)[:, None]
    radius = jnp.array(POOL_WINDOWS, dtype=jnp.int32)[None, :] // 2
    lo = jnp.clip(pos - radius, 0, S)
    hi = jnp.clip(pos + radius + 1, 0, S)
    grp = jnp.arange(POOL_GROUPS)[None, :]
    count = (hi - lo).astype(jnp.float32)
    mean = (csum[:, hi, grp, :] - csum[:, lo, grp, :]) / count[None, :, :, None]
    mixed = jnp.einsum('bsgc,gce->bsge', mean - ug, w_pool.astype(jnp.float32))
    return (mixed.reshape(B, S, POOL_WIDTH) * pool_scale.astype(jnp.float32)).astype(u.dtype)


def dilated_window_attention(q, k, v, window, dilation):
    B, S, H, Dh = q.shape
    radius = window // (2 * dilation)
    blk = radius
    L = S // dilation
    nb = -(-L // blk)
    Lp = nb * blk
    N = B * dilation

    def to_sub(t):
        t = t.reshape(B, L, dilation, H, Dh).transpose(0, 2, 1, 3, 4).reshape(N, L, H, Dh)
        return jnp.pad(t, ((0, 0), (0, Lp - L), (0, 0), (0, 0))).reshape(N, nb, blk, H, Dh)

    def with_neighbours(tb):
        tp = jnp.pad(tb, ((0, 0), (1, 1), (0, 0), (0, 0), (0, 0)))
        return jnp.concatenate([tp[:, :-2], tp[:, 1:-1], tp[:, 2:]], axis=2)

    qb = to_sub(q)
    kb = with_neighbours(to_sub(k))
    vb = with_neighbours(to_sub(v))
    qpos = jnp.arange(Lp).reshape(nb, blk)
    kpos = (jnp.arange(nb)[:, None] - 1) * blk + jnp.arange(3 * blk)[None, :]
    rel = kpos[:, None, :] - qpos[:, :, None]
    valid = (jnp.abs(rel) <= radius) & (kpos[:, None, :] >= 0) & (kpos[:, None, :] < L)
    s = jnp.einsum('nbqhd,nbkhd->nbhqk', qb, kb,
                   preferred_element_type=jnp.float32) * (HEAD_DIM ** -0.5)
    s = jnp.where(valid[None, :, None], s, NEG_INF)
    m = jnp.max(s, axis=-1, keepdims=True)
    p = jnp.exp(s - m)
    denom = jnp.sum(p, axis=-1)
    o = jnp.einsum('nbhqk,nbkhd->nbqhd', p, vb.astype(jnp.float32))
    o = o / jnp.transpose(denom, (0, 1, 3, 2))[..., None]
    lse = jnp.transpose(m[..., 0] + jnp.log(denom), (0, 1, 3, 2))

    def from_sub(t):
        t = t.reshape((N, Lp) + t.shape[3:])[:, :L]
        t = t.reshape((B, dilation, L) + t.shape[2:])
        t = jnp.swapaxes(t, 1, 2)
        return t.reshape((B, S) + t.shape[3:])

    return from_sub(o), from_sub(lse)


def attention_mixer(q, k, v, q_norm_g, k_norm_g):
    B, S, _ = q.shape
    shape = (B, S, len(ATTN_GROUPS), HEADS_PER_GROUP, HEAD_DIM)
    qh = rope(rms_norm(q.reshape(shape), q_norm_g))
    kh = rope(rms_norm(k.reshape(shape), k_norm_g))
    vh = v.reshape(shape)
    outs, lses = [], []
    for g, (window, dilation) in enumerate(ATTN_GROUPS):
        o, l = dilated_window_attention(qh[:, :, g], kh[:, :, g], vh[:, :, g], window, dilation)
        outs.append(o)
        lses.append(l)
    outs = jnp.stack(outs)
    wts = jax.nn.softmax(jnp.stack(lses), axis=0)
    out = jnp.sum(wts[..., None] * outs, axis=0)
    return out.reshape(B, S, ATTN_OUT_WIDTH).astype(q.dtype)


def conv_ffn(h, w_up, conv_w, conv_b, w_down):
    S = h.shape[1]
    gate, val = jnp.split(h @ w_up, 2, axis=-1)
    half = CONV_WIDTH // 2
    gp = jnp.pad(gate, ((0, 0), (half, half), (0, 0)))
    conv = conv_b
    for t in range(CONV_WIDTH):
        conv = conv + gp[:, t:t + S] * conv_w[t]
    return (jax.nn.gelu(conv) * val) @ w_down


def encoder_layer(x, mix_norm_g, w_in, b_gate, q_norm_g, k_norm_g, w_pool, pool_scale,
                  w_pool_out, w_attn_out, w_o, ffn_norm_g, w_up, conv_w, conv_b, w_down):
    B, S, _ = x.shape
    h = rms_norm(x, mix_norm_g)
    z = h @ w_in
    cuts = [POOL_WIDTH, POOL_WIDTH + ATTN_WIDTH, POOL_WIDTH + 2 * ATTN_WIDTH, POOL_WIDTH + 3 * ATTN_WIDTH]
    u_pool, q, k, v, g = jnp.split(z, cuts, axis=-1)
    pool_d = pool_mixer(u_pool, w_pool, pool_scale) @ w_pool_out
    attn_d = attention_mixer(q, k, v, q_norm_g, k_norm_g) @ w_attn_out
    gates = jax.nn.sigmoid((g + b_gate).astype(jnp.float32)).astype(x.dtype)
    gates = gates.reshape(B, S, N_BRANCHES, D_MODEL)
    merged = gates[:, :, 0] * pool_d + gates[:, :, 1] * attn_d
    x = x + merged @ w_o
    x = x + conv_ffn(rms_norm(x, ffn_norm_g), w_up, conv_w, conv_b, w_down)
    return x


def setup_inputs(seed: int = 0) -> dict:
    key = jax.random.key(seed)
    ks = jax.random.split(key, 18)
    f32 = jnp.float32
    nrm = lambda k, shape: jax.random.normal(k, shape, f32)
    return {
        'x_prompt': nrm(ks[0], (BATCH, SEQ, D_MODEL)),
        'x_sample': nrm(ks[1], (DEC_BATCH, DEC_SEQ, D_MODEL)),
        'mix_norm_g': 1.0 + 0.02 * nrm(ks[2], (DEPTH, D_MODEL)),
        'w_in': nrm(ks[3], (DEPTH, D_MODEL, IN_WIDTH)) * D_MODEL ** -0.5,
        'b_gate': 0.01 * nrm(ks[4], (DEPTH, N_BRANCHES * D_MODEL)),
        'q_norm_g': 1.0 + 0.02 * nrm(ks[5], (DEPTH, HEAD_DIM)),
        'k_norm_g': 1.0 + 0.02 * nrm(ks[6], (DEPTH, HEAD_DIM)),
        'w_pool': nrm(ks[7], (DEPTH, POOL_GROUPS, POOL_GROUP_DIM, POOL_GROUP_DIM)) * POOL_GROUP_DIM ** -0.5,
        'pool_scale': 1.0 + 0.1 * nrm(ks[8], (DEPTH, POOL_WIDTH)),
        'w_pool_out': nrm(ks[9], (DEPTH, POOL_WIDTH, D_MODEL)) * POOL_WIDTH ** -0.5,
        'w_attn_out': nrm(ks[10], (DEPTH, ATTN_OUT_WIDTH, D_MODEL)) * ATTN_OUT_WIDTH ** -0.5,
        'w_o': nrm(ks[11], (DEPTH, D_MODEL, D_MODEL)) * D_MODEL ** -0.5,
        'ffn_norm_g': 1.0 + 0.02 * nrm(ks[12], (DEPTH, D_MODEL)),
        'w_up': nrm(ks[13], (DEPTH, D_MODEL, 2 * D_FF)) * D_MODEL ** -0.5,
        'conv_w': nrm(ks[14], (DEPTH, CONV_WIDTH, D_FF)) * CONV_WIDTH ** -0.5,
        'conv_b': 0.01 * nrm(ks[15], (DEPTH, D_FF)),
        'w_down': nrm(ks[16], (DEPTH, D_FF, D_MODEL)) * D_FF ** -0.5,
    }


def reference(x_prompt, x_sample, mix_norm_g, w_in, b_gate, q_norm_g, k_norm_g, w_pool, pool_scale,
              w_pool_out, w_attn_out, w_o, ffn_norm_g, w_up, conv_w, conv_b, w_down):
    def trunk(x):
        for l in range(DEPTH):
            x = encoder_layer(x, mix_norm_g[l], w_in[l], b_gate[l], q_norm_g[l], k_norm_g[l],
                              w_pool[l], pool_scale[l], w_pool_out[l], w_attn_out[l], w_o[l],
                              ffn_norm_g[l], w_up[l], conv_w[l], conv_b[l], w_down[l])
        return x
    y_prompt = trunk(x_prompt)
    y_sample = trunk(x_sample)
    return (y_prompt, y_sample)
```

```python
import functools

import jax
import jax.numpy as jnp
from jax import lax
from jax.experimental import pallas as pl
from jax.experimental.pallas import tpu as pltpu

F32 = jnp.float32
BF16 = jnp.bfloat16

POOL_WINDOWS = (2, 4, 8, 16)
HEAD_DIM = 128
ATTN_GROUPS = ((128, 1), (512, 4), (2048, 16))
HEADS_PER_GROUP = 4
GROUP_WIDTH = HEADS_PER_GROUP * HEAD_DIM
ATTN_WIDTH = len(ATTN_GROUPS) * GROUP_WIDTH
ROPE_THETA = 10000.0
CONV_WIDTH = 3
EPS = 1e-6
NEG_INF = -1e30

LANES = 128
BF16_SUBLANES = 16
V7X_VMEM_BYTES = 64 * 1024 * 1024
VMEM_LIMIT_BYTES = V7X_VMEM_BYTES - 8 * 1024 * 1024

COL_TILE = GROUP_WIDTH
HALO = BF16_SUBLANES


def _tile(n, target):
    t = min(n, target)
    while n % t or t % BF16_SUBLANES:
        t -= 1
    return t


def _const_spec(shape):
    nd = len(shape)
    return pl.BlockSpec(shape, lambda *_: (0,) * nd, pipeline_mode=pl.Buffered(1))


def _params(semantics):
    return pltpu.CompilerParams(dimension_semantics=semantics, vmem_limit_bytes=VMEM_LIMIT_BYTES)


def _in_proj_kernel(x_ref, ng_ref, w_ref, qg_ref, kg_ref, cos_ref, sin_ref, z_ref, h_ref, *, q_lo):
    j = pl.program_id(1)

    @pl.when(j == 0)
    def _():
        x = x_ref[...]
        inv = lax.rsqrt(jnp.mean(x * x, axis=-1, keepdims=True) + EPS)
        h_ref[...] = (x * inv * ng_ref[...]).astype(BF16)

    acc = jnp.dot(h_ref[...], w_ref[...], preferred_element_type=F32)
    n_grp = len(ATTN_GROUPS)
    is_q = (j >= q_lo) & (j < q_lo + n_grp)
    is_k = (j >= q_lo + n_grp) & (j < q_lo + 2 * n_grp)
    is_qk = is_q | is_k

    @pl.when(is_qk)
    def _():
        gain = jnp.where(is_q, qg_ref[...], kg_ref[...])
        cos = cos_ref[...]
        sin = sin_ref[...]
        for h in range(HEADS_PER_GROUP):
            a = acc[:, h * HEAD_DIM:(h + 1) * HEAD_DIM]
            inv = lax.rsqrt(jnp.mean(a * a, axis=-1, keepdims=True) + EPS)
            a = a * inv * gain
            rot = pltpu.roll(a, HEAD_DIM // 2, axis=1)
            z_ref[:, h * HEAD_DIM:(h + 1) * HEAD_DIM] = (a * cos + rot * sin).astype(BF16)

    @pl.when(jnp.logical_not(is_qk))
    def _():
        z_ref[...] = acc.astype(BF16)


def _in_proj(x2, norm_g, w_in_b, q_norm_g, k_norm_g, cos2, sin2, seq, q_lo):
    t, d = x2.shape
    width = w_in_b.shape[1]
    tm = _tile(seq, 1024)
    pos_tiles = seq // tm
    grid = (t // tm, width // COL_TILE)
    return pl.pallas_call(
        functools.partial(_in_proj_kernel, q_lo=q_lo),
        out_shape=jax.ShapeDtypeStruct((t, width), BF16),
        grid=grid,
        in_specs=[
            pl.BlockSpec((tm, d), lambda i, j: (i, 0)),
            pl.BlockSpec((1, d), lambda i, j: (0, 0)),
            pl.BlockSpec((d, COL_TILE), lambda i, j: (0, j)),
            pl.BlockSpec((1, HEAD_DIM), lambda i, j: (0, 0)),
            pl.BlockSpec((1, HEAD_DIM), lambda i, j: (0, 0)),
            pl.BlockSpec((tm, HEAD_DIM), lambda i, j: (i % pos_tiles, 0)),
            pl.BlockSpec((tm, HEAD_DIM), lambda i, j: (i % pos_tiles, 0)),
        ],
        out_specs=pl.BlockSpec((tm, COL_TILE), lambda i, j: (i, j)),
        scratch_shapes=[pltpu.VMEM((tm, d), BF16)],
        compiler_params=_params(("parallel", "arbitrary")),
        name="in_proj",
    )(x2, norm_g, w_in_b, q_norm_g, k_norm_g, cos2, sin2)


def _attn_kernel(q_ref, kp_ref, km_ref, kn_ref, vp_ref, vm_ref, vn_ref, o_ref, lse_ref,
                 kx_ref, vx_ref, *, tq, sub, radius, length):
    i = pl.program_id(2)
    kx_ref[0:radius, :] = kp_ref[...]
    kx_ref[radius:radius + tq, :] = km_ref[...]
    kx_ref[radius + tq:, :] = kn_ref[...]
    vx_ref[0:radius, :] = vp_ref[...]
    vx_ref[radius:radius + tq, :] = vm_ref[...]
    vx_ref[radius + tq:, :] = vn_ref[...]
    nk = sub + 2 * radius
    scale = HEAD_DIM ** -0.5
    rel = lax.broadcasted_iota(jnp.int32, (sub, nk), 1) - radius - lax.broadcasted_iota(jnp.int32, (sub, nk), 0)
    band = jnp.abs(rel) <= radius
    for c in range(tq // sub):
        kpos = i * tq + c * sub - radius + lax.broadcasted_iota(jnp.int32, (sub, nk), 1)
        valid = band & (kpos >= 0) & (kpos < length)
        for h in range(HEADS_PER_GROUP):
            cols = slice(h * HEAD_DIM, (h + 1) * HEAD_DIM)
            q = q_ref[c * sub:(c + 1) * sub, cols]
            k = kx_ref[c * sub:c * sub + nk, cols]
            v = vx_ref[c * sub:c * sub + nk, cols]
            s = lax.dot_general(q, k, (((1,), (1,)), ((), ())), preferred_element_type=F32) * scale
            s = jnp.where(valid, s, NEG_INF)
            m = jnp.max(s, axis=-1, keepdims=True)
            p = jnp.exp(s - m)
            denom = jnp.sum(p, axis=-1, keepdims=True)
            o = jnp.dot(p.astype(BF16), v, preferred_element_type=F32) / denom
            o_ref[c * sub:(c + 1) * sub, cols] = o.astype(o_ref.dtype)
            lse = m + jnp.log(denom)
            lse_ref[c * sub:(c + 1) * sub, cols] = jnp.broadcast_to(lse, (sub, HEAD_DIM))


def _attention_group(z3, batch, seq, group, window, dilation, q_lo):
    radius = window // (2 * dilation)
    length = seq // dilation
    n_grp = len(ATTN_GROUPS)
    tiles_per_row = z3.shape[2] // (dilation * COL_TILE)
    tq = _tile(length, 256)
    sub = min(tq, 2 * radius)
    assert tq % sub == 0 and tq % radius == 0 and length % radius == 0
    halo_per_tile = tq // radius
    n_halo = length // radius

    def col(r, which):
        return r * tiles_per_row + q_lo + which * n_grp + group

    def main(which):
        return pl.BlockSpec((None, tq, COL_TILE), lambda b, r, i: (b, i, col(r, which)))

    def prev(which):
        return pl.BlockSpec((None, radius, COL_TILE),
                            lambda b, r, i: (b, jnp.maximum(i * halo_per_tile - 1, 0), col(r, which)))

    def nxt(which):
        return pl.BlockSpec((None, radius, COL_TILE),
                            lambda b, r, i: (b, jnp.minimum((i + 1) * halo_per_tile, n_halo - 1), col(r, which)))

    out_spec = pl.BlockSpec((None, tq, COL_TILE), lambda b, r, i: (b, i, r))
    out_shape = (batch, length, dilation * GROUP_WIDTH)
    o, lse = pl.pallas_call(
        functools.partial(_attn_kernel, tq=tq, sub=sub, radius=radius, length=length),
        out_shape=(jax.ShapeDtypeStruct(out_shape, BF16), jax.ShapeDtypeStruct(out_shape, F32)),
        grid=(batch, dilation, length // tq),
        in_specs=[main(0), prev(1), main(1), nxt(1), prev(2), main(2), nxt(2)],
        out_specs=(out_spec, out_spec),
        scratch_shapes=[pltpu.VMEM((tq + 2 * radius, COL_TILE), BF16),
                        pltpu.VMEM((tq + 2 * radius, COL_TILE), BF16)],
        compiler_params=_params(("parallel", "parallel", "arbitrary")),
        name=f"attn_g{group}",
    )(z3, z3, z3, z3, z3, z3, z3)
    return o.reshape(batch * seq, GROUP_WIDTH), lse.reshape(batch * seq, GROUP_WIDTH)


def _merge_kernel(x_ref, g_ref, up_ref, um_ref, un_ref,
                  o0_ref, o1_ref, o2_ref, l0_ref, l1_ref, l2_ref,
                  bg_ref, wp_ref, ps_ref, wpo_ref, wao_ref, wo_ref, fg_ref,
                  x1_ref, h2_ref, ux_ref, *, tm, seq):
    i = pl.program_id(0)
    d = x_ref.shape[1]
    pool_w = um_ref.shape[1]
    n_pool = len(POOL_WINDOWS)
    gdim = pool_w // n_pool

    pos0 = (i * tm) % seq
    pos_p = pos0 - HALO + lax.broadcasted_iota(jnp.int32, (HALO, 1), 0)
    pos_n = pos0 + tm + lax.broadcasted_iota(jnp.int32, (HALO, 1), 0)
    ux_ref[0:HALO, :] = jnp.where(pos_p >= 0, up_ref[...].astype(F32), 0.0)
    ux_ref[HALO:HALO + tm, :] = um_ref[...].astype(F32)
    ux_ref[HALO + tm:, :] = jnp.where(pos_n < seq, un_ref[...].astype(F32), 0.0)

    pos = pos0 + lax.broadcasted_iota(jnp.int32, (tm, 1), 0)
    mixed = []
    for g, window in enumerate(POOL_WINDOWS):
        rad = window // 2
        cols = slice(g * gdim, (g + 1) * gdim)
        u = ux_ref[HALO:HALO + tm, cols]
        tot = u
        for t in range(1, rad + 1):
            tot = tot + ux_ref[HALO - t:HALO - t + tm, cols] + ux_ref[HALO + t:HALO + t + tm, cols]
        count = (jnp.minimum(pos + rad + 1, seq) - jnp.maximum(pos - rad, 0)).astype(F32)
        centred = tot / count - u
        mg = jnp.dot(centred.astype(BF16), wp_ref[g], preferred_element_type=F32)
        mixed.append((mg * ps_ref[:, cols]).astype(BF16))
    pool_d = jnp.dot(jnp.concatenate(mixed, axis=1), wpo_ref[...], preferred_element_type=F32)

    l0, l1, l2 = l0_ref[...], l1_ref[...], l2_ref[...]
    lmax = jnp.maximum(jnp.maximum(l0, l1), l2)
    e0, e1, e2 = jnp.exp(l0 - lmax), jnp.exp(l1 - lmax), jnp.exp(l2 - lmax)
    attn = (e0 * o0_ref[...].astype(F32) + e1 * o1_ref[...].astype(F32)
            + e2 * o2_ref[...].astype(F32)) / (e0 + e1 + e2)
    attn_d = jnp.dot(attn.astype(BF16), wao_ref[...], preferred_element_type=F32)

    gates = jax.nn.sigmoid(g_ref[...].astype(F32) + bg_ref[...])
    merged = gates[:, :d] * pool_d + gates[:, d:] * attn_d
    x1 = x_ref[...] + jnp.dot(merged.astype(BF16), wo_ref[...], preferred_element_type=F32)
    x1_ref[...] = x1
    inv = lax.rsqrt(jnp.mean(x1 * x1, axis=-1, keepdims=True) + EPS)
    h2_ref[...] = (x1 * inv * fg_ref[...]).astype(BF16)


def _merge(x2, z, attn_outs, b_gate, w_pool_b, pool_scale, w_pool_out_b, w_attn_out_b, w_o_b,
           ffn_norm_g, seq):
    t, d = x2.shape
    pool_w = w_pool_out_b.shape[0]
    tm = _tile(seq, 256)
    halo_per_tile = tm // HALO
    n_halo = t // HALO
    u_col = (2 * d) // pool_w

    row = lambda i: (i, 0)
    grp = pl.BlockSpec((tm, GROUP_WIDTH), row)
    (o0, l0), (o1, l1), (o2, l2) = attn_outs
    return pl.pallas_call(
        functools.partial(_merge_kernel, tm=tm, seq=seq),
        out_shape=(jax.ShapeDtypeStruct((t, d), F32), jax.ShapeDtypeStruct((t, d), BF16)),
        grid=(t // tm,),
        in_specs=[
            pl.BlockSpec((tm, d), row),
            pl.BlockSpec((tm, 2 * d), row),
            pl.BlockSpec((HALO, pool_w), lambda i: (jnp.maximum(i * halo_per_tile - 1, 0), u_col)),
            pl.BlockSpec((tm, pool_w), lambda i: (i, u_col)),
            pl.BlockSpec((HALO, pool_w), lambda i: (jnp.minimum((i + 1) * halo_per_tile, n_halo - 1), u_col)),
            grp, grp, grp, grp, grp, grp,
            _const_spec(b_gate.shape),
            _const_spec(w_pool_b.shape),
            _const_spec(pool_scale.shape),
            _const_spec(w_pool_out_b.shape),
            _const_spec(w_attn_out_b.shape),
            _const_spec(w_o_b.shape),
            _const_spec(ffn_norm_g.shape),
        ],
        out_specs=(pl.BlockSpec((tm, d), row), pl.BlockSpec((tm, d), row)),
        scratch_shapes=[pltpu.VMEM((tm + 2 * HALO, pool_w), F32)],
        compiler_params=_params(("parallel",)),
        name="merge",
    )(x2, z, z, z, z, o0, o1, o2, l0, l1, l2,
      b_gate, w_pool_b, pool_scale, w_pool_out_b, w_attn_out_b, w_o_b, ffn_norm_g)


def _ffn_kernel(hp_ref, hm_ref, hn_ref, x1_ref, wg_ref, wv_ref, cw_ref, cb_ref, wd_ref,
                y_ref, hx_ref, *, tm, seq):
    i = pl.program_id(0)
    f = pl.program_id(1)

    @pl.when(f == 0)
    def _():
        pos0 = (i * tm) % seq
        pos_p = pos0 - HALO + lax.broadcasted_iota(jnp.int32, (HALO, 1), 0)
        pos_n = pos0 + tm + lax.broadcasted_iota(jnp.int32, (HALO, 1), 0)
        hx_ref[0:HALO, :] = jnp.where(pos_p >= 0, hp_ref[...], jnp.zeros_like(hp_ref[...]))
        hx_ref[HALO:HALO + tm, :] = hm_ref[...]
        hx_ref[HALO + tm:, :] = jnp.where(pos_n < seq, hn_ref[...], jnp.zeros_like(hn_ref[...]))
        y_ref[...] = x1_ref[...]

    gate = jnp.dot(hx_ref[...], wg_ref[...], preferred_element_type=F32)
    val = jnp.dot(hm_ref[...], wv_ref[...], preferred_element_type=F32)
    half = CONV_WIDTH // 2
    conv = cb_ref[...]
    for tap in range(CONV_WIDTH):
        lo = HALO - half + tap
        conv = conv + gate[lo:lo + tm, :] * cw_ref[tap:tap + 1, :]
    act = jax.nn.gelu(conv) * val
    y_ref[...] += jnp.dot(act.astype(BF16), wd_ref[...], preferred_element_type=F32)


def _ffn(h2, x1, w_up_b, conv_w, conv_b, w_down_b, seq):
    t, d = x1.shape
    d_ff = w_down_b.shape[0]
    tm = _tile(seq, 512)
    tf = 512 if d_ff % 512 == 0 else 256
    nf = d_ff // tf
    halo_per_tile = tm // HALO
    n_halo = t // HALO
    return pl.pallas_call(
        functools.partial(_ffn_kernel, tm=tm, seq=seq),
        out_shape=jax.ShapeDtypeStruct((t, d), F32),
        grid=(t // tm, nf),
        in_specs=[
            pl.BlockSpec((HALO, d), lambda i, f: (jnp.maximum(i * halo_per_tile - 1, 0), 0)),
            pl.BlockSpec((tm, d), lambda i, f: (i, 0)),
            pl.BlockSpec((HALO, d), lambda i, f: (jnp.minimum((i + 1) * halo_per_tile, n_halo - 1), 0)),
            pl.BlockSpec((tm, d), lambda i, f: (i, 0)),
            pl.BlockSpec((d, tf), lambda i, f: (0, f)),
            pl.BlockSpec((d, tf), lambda i, f: (0, nf + f)),
            pl.BlockSpec((CONV_WIDTH, tf), lambda i, f: (0, f)),
            pl.BlockSpec((1, tf), lambda i, f: (0, f)),
            pl.BlockSpec((tf, d), lambda i, f: (f, 0)),
        ],
        out_specs=pl.BlockSpec((tm, d), lambda i, f: (i, 0)),
        scratch_shapes=[pltpu.VMEM((tm + 2 * HALO, d), BF16)],
        compiler_params=_params(("parallel", "arbitrary")),
        name="ffn",
    )(h2, h2, h2, x1, w_up_b, w_up_b, conv_w, conv_b, w_down_b)


def _rope_tables(seq):
    half = HEAD_DIM // 2
    inv_freq = ROPE_THETA ** (-jnp.arange(half, dtype=F32) / half)
    ang = jnp.arange(seq, dtype=F32)[:, None] * inv_freq[None, :]
    cos, sin = jnp.cos(ang), jnp.sin(ang)
    return jnp.concatenate([cos, cos], axis=-1), jnp.concatenate([-sin, sin], axis=-1)


def _layer(x, p):
    batch, seq, d = x.shape
    x2 = x.reshape(batch * seq, d)
    cos2, sin2 = _rope_tables(seq)
    q_lo = (2 * d + p["pool_w"]) // COL_TILE
    z = _in_proj(x2, p["mix_norm_g"], p["w_in"], p["q_norm_g"], p["k_norm_g"], cos2, sin2, seq, q_lo)
    attn_outs = []
    for g, (window, dilation) in enumerate(ATTN_GROUPS):
        z3 = z.reshape(batch, seq // dilation, dilation * z.shape[1])
        attn_outs.append(_attention_group(z3, batch, seq, g, window, dilation, q_lo))
    x1, h2 = _merge(x2, z, attn_outs, p["b_gate"], p["w_pool"], p["pool_scale"], p["w_pool_out"],
                    p["w_attn_out"], p["w_o"], p["ffn_norm_g"], seq)
    y = _ffn(h2, x1, p["w_up"], p["conv_w"], p["conv_b"], p["w_down"], seq)
    return y.reshape(batch, seq, d)


def kernel(x_prompt, x_sample, mix_norm_g, w_in, b_gate, q_norm_g, k_norm_g, w_pool, pool_scale,
           w_pool_out, w_attn_out, w_o, ffn_norm_g, w_up, conv_w, conv_b, w_down):
    depth = w_in.shape[0]
    pool_w = w_pool_out.shape[1]
    layers = []
    for l in range(depth):
        w = w_in[l]
        w_perm = jnp.concatenate([w[:, pool_w + 3 * ATTN_WIDTH:], w[:, :pool_w + 3 * ATTN_WIDTH]], axis=1)
        layers.append(dict(
            pool_w=pool_w,
            mix_norm_g=mix_norm_g[l][None, :], w_in=w_perm.astype(BF16), b_gate=b_gate[l][None, :],
            q_norm_g=q_norm_g[l][None, :], k_norm_g=k_norm_g[l][None, :],
            w_pool=w_pool[l].astype(BF16), pool_scale=pool_scale[l][None, :],
            w_pool_out=w_pool_out[l].astype(BF16), w_attn_out=w_attn_out[l].astype(BF16),
            w_o=w_o[l].astype(BF16), ffn_norm_g=ffn_norm_g[l][None, :],
            w_up=w_up[l].astype(BF16), conv_w=conv_w[l], conv_b=conv_b[l][None, :],
            w_down=w_down[l].astype(BF16)))

    def trunk(x):
        for p in layers:
            x = _layer(x, p)
        return x

    return (trunk(x_prompt), trunk(x_sample))
```

```python
import functools

import jax
import jax.numpy as jnp
from jax import lax
from jax.experimental import pallas as pl
from jax.experimental.pallas import tpu as pltpu

F32 = jnp.float32
BF16 = jnp.bfloat16

POOL_WINDOWS = (2, 4, 8, 16)
HEAD_DIM = 128
ATTN_GROUPS = ((128, 1), (512, 4), (2048, 16))
HEADS_PER_GROUP = 4
GROUP_WIDTH = HEADS_PER_GROUP * HEAD_DIM
ATTN_WIDTH = len(ATTN_GROUPS) * GROUP_WIDTH
ROPE_THETA = 10000.0
CONV_WIDTH = 3
EPS = 1e-6
NEG_INF = -1e30

LANES = 128
BF16_SUBLANES = 16
V7X_VMEM_BYTES = 64 * 1024 * 1024
VMEM_LIMIT_BYTES = V7X_VMEM_BYTES - 8 * 1024 * 1024

COL_TILE = GROUP_WIDTH
HALO = BF16_SUBLANES


def _tile(n, target):
    t = min(n, target)
    while n % t or t % BF16_SUBLANES:
        t -= 1
    return t


def _const_spec(shape):
    nd = len(shape)
    return pl.BlockSpec(shape, lambda *_: (0,) * nd, pipeline_mode=pl.Buffered(1))


def _params(semantics):
    return pltpu.CompilerParams(dimension_semantics=semantics, vmem_limit_bytes=VMEM_LIMIT_BYTES)


def _in_proj_kernel(x_ref, ng_ref, w_ref, qg_ref, kg_ref, cos_ref, sin_ref,
                    zr_ref, qkv0_ref, qkv1_ref, qkv2_ref, h_ref, de_ref, *, q_lo):
    j = pl.program_id(1)
    tm = x_ref.shape[0]
    n_grp = len(ATTN_GROUPS)

    @pl.when(j == 0)
    def _():
        x = x_ref[...]
        inv = lax.rsqrt(jnp.mean(x * x, axis=-1, keepdims=True) + EPS)
        h_ref[...] = (x * inv * ng_ref[...]).astype(BF16)

    acc = jnp.dot(h_ref[...], w_ref[...], preferred_element_type=F32)

    @pl.when(j < q_lo)
    def _():
        zr_ref[...] = acc.astype(BF16)

    def normed_rope(gain):
        cos = cos_ref[...]
        sin = sin_ref[...]
        heads = []
        for h in range(HEADS_PER_GROUP):
            a = acc[:, h * HEAD_DIM:(h + 1) * HEAD_DIM]
            inv = lax.rsqrt(jnp.mean(a * a, axis=-1, keepdims=True) + EPS)
            a = a * inv * gain
            rot = pltpu.roll(a, HEAD_DIM // 2, axis=1)
            heads.append(a * cos + rot * sin)
        return heads

    for which in range(3):
        for g, (out_ref, (_, dilation)) in enumerate(zip((qkv0_ref, qkv1_ref, qkv2_ref), ATTN_GROUPS)):
            @pl.when(j == q_lo + which * n_grp + g)
            def _(which=which, out_ref=out_ref, dilation=dilation):
                if which == 2:
                    heads = [acc[:, h * HEAD_DIM:(h + 1) * HEAD_DIM] for h in range(HEADS_PER_GROUP)]
                else:
                    heads = normed_rope(qg_ref[...] if which == 0 else kg_ref[...])
                for h, a in enumerate(heads):
                    cols = slice(h * HEAD_DIM, (h + 1) * HEAD_DIM)
                    if dilation == 1:
                        out_ref[which, 0, :, cols] = a.astype(BF16)
                    else:
                        de_ref[h] = a
                        for r in range(dilation):
                            rows = de_ref[h, pl.ds(r, tm // dilation, stride=dilation), :]
                            out_ref[which, r, :, cols] = rows.astype(BF16)


def _in_proj(x2, norm_g, w_in_b, q_norm_g, k_norm_g, cos2, sin2, batch, seq, q_lo):
    t, d = x2.shape
    width = w_in_b.shape[1]
    tm = _tile(seq, 1024)
    pos_tiles = seq // tm
    grid = (t // tm, width // COL_TILE)
    qkv_shapes, qkv_specs = [], []
    for _, dilation in ATTN_GROUPS:
        assert tm % (dilation * BF16_SUBLANES) == 0
        qkv_shapes.append(jax.ShapeDtypeStruct((3, batch, dilation, seq // dilation, GROUP_WIDTH), BF16))
        qkv_specs.append(pl.BlockSpec((3, None, dilation, tm // dilation, GROUP_WIDTH),
                                      lambda i, j: (0, i // pos_tiles, 0, i % pos_tiles, 0)))
    return pl.pallas_call(
        functools.partial(_in_proj_kernel, q_lo=q_lo),
        out_shape=[jax.ShapeDtypeStruct((t, q_lo * COL_TILE), BF16)] + qkv_shapes,
        grid=grid,
        in_specs=[
            pl.BlockSpec((tm, d), lambda i, j: (i, 0)),
            pl.BlockSpec((1, d), lambda i, j: (0, 0)),
            pl.BlockSpec((d, COL_TILE), lambda i, j: (0, j)),
            pl.BlockSpec((1, HEAD_DIM), lambda i, j: (0, 0)),
            pl.BlockSpec((1, HEAD_DIM), lambda i, j: (0, 0)),
            pl.BlockSpec((tm, HEAD_DIM), lambda i, j: (i % pos_tiles, 0)),
            pl.BlockSpec((tm, HEAD_DIM), lambda i, j: (i % pos_tiles, 0)),
        ],
        out_specs=[pl.BlockSpec((tm, COL_TILE), lambda i, j: (i, jnp.minimum(j, q_lo - 1)))] + qkv_specs,
        scratch_shapes=[pltpu.VMEM((tm, d), BF16), pltpu.VMEM((HEADS_PER_GROUP, tm, HEAD_DIM), F32)],
        compiler_params=_params(("parallel", "arbitrary")),
        name="in_proj",
    )(x2, norm_g, w_in_b, q_norm_g, k_norm_g, cos2, sin2)


def _attn_kernel(q_ref, kp_ref, km_ref, kn_ref, vp_ref, vm_ref, vn_ref, o_ref, lse_ref,
                 kx_ref, vx_ref, *, tq, sub, radius, length):
    i = pl.program_id(2)
    kx_ref[0:radius, :] = kp_ref[...]
    kx_ref[radius:radius + tq, :] = km_ref[...]
    kx_ref[radius + tq:, :] = kn_ref[...]
    vx_ref[0:radius, :] = vp_ref[...]
    vx_ref[radius:radius + tq, :] = vm_ref[...]
    vx_ref[radius + tq:, :] = vn_ref[...]
    nk = sub + 2 * radius
    scale = HEAD_DIM ** -0.5
    rel = lax.broadcasted_iota(jnp.int32, (sub, nk), 1) - radius - lax.broadcasted_iota(jnp.int32, (sub, nk), 0)
    band = jnp.abs(rel) <= radius
    for c in range(tq // sub):
        kpos = i * tq + c * sub - radius + lax.broadcasted_iota(jnp.int32, (sub, nk), 1)
        valid = band & (kpos >= 0) & (kpos < length)
        for h in range(HEADS_PER_GROUP):
            cols = slice(h * HEAD_DIM, (h + 1) * HEAD_DIM)
            q = q_ref[c * sub:(c + 1) * sub, cols]
            k = kx_ref[c * sub:c * sub + nk, cols]
            v = vx_ref[c * sub:c * sub + nk, cols]
            s = lax.dot_general(q, k, (((1,), (1,)), ((), ())), preferred_element_type=F32) * scale
            s = jnp.where(valid, s, NEG_INF)
            m = jnp.max(s, axis=-1, keepdims=True)
            p = jnp.exp(s - m)
            denom = jnp.sum(p, axis=-1, keepdims=True)
            o = jnp.dot(p.astype(BF16), v, preferred_element_type=F32) / denom
            o_ref[c * sub:(c + 1) * sub, cols] = o.astype(o_ref.dtype)
            lse = m + jnp.log(denom)
            lse_ref[c * sub:(c + 1) * sub, cols] = jnp.broadcast_to(lse, (sub, HEAD_DIM))


def _attention_group(qkv, group, window, dilation):
    _, batch, _, length, _ = qkv.shape
    radius = window // (2 * dilation)
    tq = _tile(length, 256)
    sub = min(tq, 2 * radius)
    assert tq % sub == 0 and tq % radius == 0 and length % radius == 0
    halo_per_tile = tq // radius
    n_halo = length // radius

    def main(which):
        return pl.BlockSpec((None, None, None, tq, GROUP_WIDTH), lambda b, r, i: (which, b, r, i, 0))

    def prev(which):
        return pl.BlockSpec((None, None, None, radius, GROUP_WIDTH),
                            lambda b, r, i: (which, b, r, jnp.maximum(i * halo_per_tile - 1, 0), 0))

    def nxt(which):
        return pl.BlockSpec((None, None, None, radius, GROUP_WIDTH),
                            lambda b, r, i: (which, b, r, jnp.minimum((i + 1) * halo_per_tile, n_halo - 1), 0))

    out_spec = pl.BlockSpec((None, None, tq, GROUP_WIDTH), lambda b, r, i: (b, r, i, 0))
    out_shape = (batch, dilation, length, GROUP_WIDTH)
    return pl.pallas_call(
        functools.partial(_attn_kernel, tq=tq, sub=sub, radius=radius, length=length),
        out_shape=(jax.ShapeDtypeStruct(out_shape, BF16), jax.ShapeDtypeStruct(out_shape, F32)),
        grid=(batch, dilation, length // tq),
        in_specs=[main(0), prev(1), main(1), nxt(1), prev(2), main(2), nxt(2)],
        out_specs=(out_spec, out_spec),
        scratch_shapes=[pltpu.VMEM((tq + 2 * radius, GROUP_WIDTH), BF16),
                        pltpu.VMEM((tq + 2 * radius, GROUP_WIDTH), BF16)],
        compiler_params=_params(("parallel", "parallel", "arbitrary")),
        name=f"attn_g{group}",
    )(qkv, qkv, qkv, qkv, qkv, qkv, qkv)


def _merge_kernel(x_ref, g_ref, up_ref, um_ref, un_ref,
                  o0_ref, o1_ref, o2_ref, l0_ref, l1_ref, l2_ref,
                  bg_ref, wp_ref, ps_ref, wpo_ref, wao_ref, wo_ref, fg_ref,
                  x1_ref, h2_ref, ux_ref, oi_ref, li_ref, *, tm, seq):
    i = pl.program_id(0)
    d = x_ref.shape[1]
    pool_w = um_ref.shape[1]
    n_pool = len(POOL_WINDOWS)
    gdim = pool_w // n_pool

    pos0 = (i * tm) % seq
    pos_p = pos0 - HALO + lax.broadcasted_iota(jnp.int32, (HALO, 1), 0)
    pos_n = pos0 + tm + lax.broadcasted_iota(jnp.int32, (HALO, 1), 0)
    ux_ref[0:HALO, :] = jnp.where(pos_p >= 0, up_ref[...].astype(F32), 0.0)
    ux_ref[HALO:HALO + tm, :] = um_ref[...].astype(F32)
    ux_ref[HALO + tm:, :] = jnp.where(pos_n < seq, un_ref[...].astype(F32), 0.0)

    pos = pos0 + lax.broadcasted_iota(jnp.int32, (tm, 1), 0)
    mixed = []
    for g, window in enumerate(POOL_WINDOWS):
        rad = window // 2
        cols = slice(g * gdim, (g + 1) * gdim)
        u = ux_ref[HALO:HALO + tm, cols]
        tot = u
        for t in range(1, rad + 1):
            tot = tot + ux_ref[HALO - t:HALO - t + tm, cols] + ux_ref[HALO + t:HALO + t + tm, cols]
        count = (jnp.minimum(pos + rad + 1, seq) - jnp.maximum(pos - rad, 0)).astype(F32)
        centred = tot / count - u
        mg = jnp.dot(centred.astype(BF16), wp_ref[g], preferred_element_type=F32)
        mixed.append((mg * ps_ref[:, cols]).astype(BF16))
    pool_d = jnp.dot(jnp.concatenate(mixed, axis=1), wpo_ref[...], preferred_element_type=F32)

    outs, lses = [], []
    for g, (o_ref, l_ref) in enumerate(((o0_ref, l0_ref), (o1_ref, l1_ref), (o2_ref, l2_ref))):
        dilation = o_ref.shape[0]
        if dilation == 1:
            outs.append(o_ref[0].astype(F32))
            lses.append(l_ref[0])
        else:
            for h in range(HEADS_PER_GROUP):
                cols = slice(h * HEAD_DIM, (h + 1) * HEAD_DIM)
                for r in range(dilation):
                    rows = pl.ds(r, tm // dilation, stride=dilation)
                    oi_ref[g, h, rows, :] = o_ref[r, :, cols].astype(F32)
                    li_ref[g, h, rows, :] = l_ref[r, :, cols]
            outs.append(jnp.concatenate([oi_ref[g, h] for h in range(HEADS_PER_GROUP)], axis=1))
            lses.append(jnp.concatenate([li_ref[g, h] for h in range(HEADS_PER_GROUP)], axis=1))
    l0, l1, l2 = lses
    lmax = jnp.maximum(jnp.maximum(l0, l1), l2)
    e0, e1, e2 = jnp.exp(l0 - lmax), jnp.exp(l1 - lmax), jnp.exp(l2 - lmax)
    attn = (e0 * outs[0] + e1 * outs[1] + e2 * outs[2]) / (e0 + e1 + e2)
    attn_d = jnp.dot(attn.astype(BF16), wao_ref[...], preferred_element_type=F32)

    gates = jax.nn.sigmoid(g_ref[...].astype(F32) + bg_ref[...])
    merged = gates[:, :d] * pool_d + gates[:, d:] * attn_d
    x1 = x_ref[...] + jnp.dot(merged.astype(BF16), wo_ref[...], preferred_element_type=F32)
    x1_ref[...] = x1
    inv = lax.rsqrt(jnp.mean(x1 * x1, axis=-1, keepdims=True) + EPS)
    h2_ref[...] = (x1 * inv * fg_ref[...]).astype(BF16)


def _merge(x2, z, attn_outs, b_gate, w_pool_b, pool_scale, w_pool_out_b, w_attn_out_b, w_o_b,
           ffn_norm_g, seq):
    t, d = x2.shape
    pool_w = w_pool_out_b.shape[0]
    tm = _tile(seq, 256)
    halo_per_tile = tm // HALO
    n_halo = t // HALO
    u_col = (2 * d) // pool_w

    row = lambda i: (i, 0)
    pos_tiles = seq // tm
    (o0, l0), (o1, l1), (o2, l2) = attn_outs
    grp = []
    for _, dilation in ATTN_GROUPS:
        assert tm % (dilation * BF16_SUBLANES) == 0
        grp.append(pl.BlockSpec((None, dilation, tm // dilation, GROUP_WIDTH),
                                lambda i: (i // pos_tiles, 0, i % pos_tiles, 0)))
    return pl.pallas_call(
        functools.partial(_merge_kernel, tm=tm, seq=seq),
        out_shape=(jax.ShapeDtypeStruct((t, d), F32), jax.ShapeDtypeStruct((t, d), BF16)),
        grid=(t // tm,),
        in_specs=[
            pl.BlockSpec((tm, d), row),
            pl.BlockSpec((tm, 2 * d), row),
            pl.BlockSpec((HALO, pool_w), lambda i: (jnp.maximum(i * halo_per_tile - 1, 0), u_col)),
            pl.BlockSpec((tm, pool_w), lambda i: (i, u_col)),
            pl.BlockSpec((HALO, pool_w), lambda i: (jnp.minimum((i + 1) * halo_per_tile, n_halo - 1), u_col)),
            grp[0], grp[1], grp[2], grp[0], grp[1], grp[2],
            _const_spec(b_gate.shape),
            _const_spec(w_pool_b.shape),
            _const_spec(pool_scale.shape),
            _const_spec(w_pool_out_b.shape),
            _const_spec(w_attn_out_b.shape),
            _const_spec(w_o_b.shape),
            _const_spec(ffn_norm_g.shape),
        ],
        out_specs=(pl.BlockSpec((tm, d), row), pl.BlockSpec((tm, d), row)),
        scratch_shapes=[pltpu.VMEM((tm + 2 * HALO, pool_w), F32),
                        pltpu.VMEM((len(ATTN_GROUPS), HEADS_PER_GROUP, tm, HEAD_DIM), F32),
                        pltpu.VMEM((len(ATTN_GROUPS), HEADS_PER_GROUP, tm, HEAD_DIM), F32)],
        compiler_params=_params(("parallel",)),
        name="merge",
    )(x2, z, z, z, z, o0, o1, o2, l0, l1, l2,
      b_gate, w_pool_b, pool_scale, w_pool_out_b, w_attn_out_b, w_o_b, ffn_norm_g)


def _ffn_kernel(hp_ref, hm_ref, hn_ref, x1_ref, wg_ref, wv_ref, cw_ref, cb_ref, wd_ref,
                y_ref, hx_ref, *, tm, seq):
    i = pl.program_id(0)
    f = pl.program_id(1)

    @pl.when(f == 0)
    def _():
        pos0 = (i * tm) % seq
        pos_p = pos0 - HALO + lax.broadcasted_iota(jnp.int32, (HALO, 1), 0)
        pos_n = pos0 + tm + lax.broadcasted_iota(jnp.int32, (HALO, 1), 0)
        hx_ref[0:HALO, :] = jnp.where(pos_p >= 0, hp_ref[...], jnp.zeros_like(hp_ref[...]))
        hx_ref[HALO:HALO + tm, :] = hm_ref[...]
        hx_ref[HALO + tm:, :] = jnp.where(pos_n < seq, hn_ref[...], jnp.zeros_like(hn_ref[...]))
        y_ref[...] = x1_ref[...]

    gate = jnp.dot(hx_ref[...], wg_ref[...], preferred_element_type=F32)
    val = jnp.dot(hm_ref[...], wv_ref[...], preferred_element_type=F32)
    half = CONV_WIDTH // 2
    conv = cb_ref[...]
    for tap in range(CONV_WIDTH):
        lo = HALO - half + tap
        conv = conv + gate[lo:lo + tm, :] * cw_ref[tap:tap + 1, :]
    act = jax.nn.gelu(conv) * val
    y_ref[...] += jnp.dot(act.astype(BF16), wd_ref[...], preferred_element_type=F32)


def _ffn(h2, x1, w_up_b, conv_w, conv_b, w_down_b, seq):
    t, d = x1.shape
    d_ff = w_down_b.shape[0]
    tm = _tile(seq, 512)
    tf = 512 if d_ff % 512 == 0 else 256
    nf = d_ff // tf
    halo_per_tile = tm // HALO
    n_halo = t // HALO
    return pl.pallas_call(
        functools.partial(_ffn_kernel, tm=tm, seq=seq),
        out_shape=jax.ShapeDtypeStruct((t, d), F32),
        grid=(t // tm, nf),
        in_specs=[
            pl.BlockSpec((HALO, d), lambda i, f: (jnp.maximum(i * halo_per_tile - 1, 0), 0)),
            pl.BlockSpec((tm, d), lambda i, f: (i, 0)),
            pl.BlockSpec((HALO, d), lambda i, f: (jnp.minimum((i + 1) * halo_per_tile, n_halo - 1), 0)),
            pl.BlockSpec((tm, d), lambda i, f: (i, 0)),
            pl.BlockSpec((d, tf), lambda i, f: (0, f)),
            pl.BlockSpec((d, tf), lambda i, f: (0, nf + f)),
            pl.BlockSpec((CONV_WIDTH, tf), lambda i, f: (0, f)),
            pl.BlockSpec((1, tf), lambda i, f: (0, f)),
            pl.BlockSpec((tf, d), lambda i, f: (f, 0)),
        ],
        out_specs=pl.BlockSpec((tm, d), lambda i, f: (i, 0)),
        scratch_shapes=[pltpu.VMEM((tm + 2 * HALO, d), BF16)],
        compiler_params=_params(("parallel", "arbitrary")),
        name="ffn",
    )(h2, h2, h2, x1, w_up_b, w_up_b, conv_w, conv_b, w_down_b)


def _rope_tables(seq):
    half = HEAD_DIM // 2
    inv_freq = ROPE_THETA ** (-jnp.arange(half, dtype=F32) / half)
    ang = jnp.arange(seq, dtype=F32)[:, None] * inv_freq[None, :]
    cos, sin = jnp.cos(ang), jnp.sin(ang)
    return jnp.concatenate([cos, cos], axis=-1), jnp.concatenate([-sin, sin], axis=-1)


def _layer(x, p):
    batch, seq, d = x.shape
    x2 = x.reshape(batch * seq, d)
    cos2, sin2 = _rope_tables(seq)
    q_lo = (2 * d + p["pool_w"]) // COL_TILE
    z, *qkv = _in_proj(x2, p["mix_norm_g"], p["w_in"], p["q_norm_g"], p["k_norm_g"], cos2, sin2,
                       batch, seq, q_lo)
    attn_outs = [_attention_group(qkv[g], g, window, dilation)
                 for g, (window, dilation) in enumerate(ATTN_GROUPS)]
    x1, h2 = _merge(x2, z, attn_outs, p["b_gate"], p["w_pool"], p["pool_scale"], p["w_pool_out"],
                    p["w_attn_out"], p["w_o"], p["ffn_norm_g"], seq)
    y = _ffn(h2, x1, p["w_up"], p["conv_w"], p["conv_b"], p["w_down"], seq)
    return y.reshape(batch, seq, d)


def kernel(x_prompt, x_sample, mix_norm_g, w_in, b_gate, q_norm_g, k_norm_g, w_pool, pool_scale,
           w_pool_out, w_attn_out, w_o, ffn_norm_g, w_up, conv_w, conv_b, w_down):
    depth = w_in.shape[0]
    pool_w = w_pool_out.shape[1]
    layers = []
    for l in range(depth):
        w = w_in[l]
        w_perm = jnp.concatenate([w[:, pool_w + 3 * ATTN_WIDTH:], w[:, :pool_w + 3 * ATTN_WIDTH]], axis=1)
        layers.append(dict(
            pool_w=pool_w,
            mix_norm_g=mix_norm_g[l][None, :], w_in=w_perm.astype(BF16), b_gate=b_gate[l][None, :],
            q_norm_g=q_norm_g[l][None, :], k_norm_g=k_norm_g[l][None, :],
            w_pool=w_pool[l].astype(BF16), pool_scale=pool_scale[l][None, :],
            w_pool_out=w_pool_out[l].astype(BF16), w_attn_out=w_attn_out[l].astype(BF16),
            w_o=w_o[l].astype(BF16), ffn_norm_g=ffn_norm_g[l][None, :],
            w_up=w_up[l].astype(BF16), conv_w=conv_w[l], conv_b=conv_b[l][None, :],
            w_down=w_down[l].astype(BF16)))

    def trunk(x):
        for p in layers:
            x = _layer(x, p)
        return x

    return (trunk(x_prompt), trunk(x_sample))
```

```python
import functools

import jax
import jax.numpy as jnp
from jax import lax
from jax.experimental import pallas as pl
from jax.experimental.pallas import tpu as pltpu

F32 = jnp.float32
BF16 = jnp.bfloat16

POOL_WINDOWS = (2, 4, 8, 16)
HEAD_DIM = 128
ATTN_GROUPS = ((128, 1), (512, 4), (2048, 16))
HEADS_PER_GROUP = 4
GROUP_WIDTH = HEADS_PER_GROUP * HEAD_DIM
ATTN_WIDTH = len(ATTN_GROUPS) * GROUP_WIDTH
ROPE_THETA = 10000.0
CONV_WIDTH = 3
EPS = 1e-6
NEG_INF = -1e30

LANES = 128
BF16_SUBLANES = 16
V7X_VMEM_BYTES = 64 * 1024 * 1024
VMEM_LIMIT_BYTES = V7X_VMEM_BYTES - 8 * 1024 * 1024

COL_TILE = GROUP_WIDTH
HALO = BF16_SUBLANES


def _tile(n, target):
    t = min(n, target)
    while n % t or t % BF16_SUBLANES:
        t -= 1
    return t


def _const_spec(shape):
    nd = len(shape)
    return pl.BlockSpec(shape, lambda *_: (0,) * nd, pipeline_mode=pl.Buffered(1))


def _params(semantics):
    return pltpu.CompilerParams(dimension_semantics=semantics, vmem_limit_bytes=VMEM_LIMIT_BYTES)


def _in_proj_kernel(x_ref, ng_ref, w_ref, qg_ref, kg_ref, cos_ref, sin_ref,
                    zr_ref, qkv0_ref, qkv1_ref, qkv2_ref, h_ref, de_ref, *, q_lo):
    j = pl.program_id(1)
    tm = x_ref.shape[0]
    n_grp = len(ATTN_GROUPS)

    @pl.when(j == 0)
    def _():
        x = x_ref[...]
        inv = lax.rsqrt(jnp.mean(x * x, axis=-1, keepdims=True) + EPS)
        h_ref[...] = (x * inv * ng_ref[...]).astype(BF16)

    acc = jnp.dot(h_ref[...], w_ref[...], preferred_element_type=F32)

    @pl.when(j < q_lo)
    def _():
        zr_ref[...] = acc.astype(BF16)

    def normed_rope(gain):
        cos = cos_ref[...]
        sin = sin_ref[...]
        heads = []
        for h in range(HEADS_PER_GROUP):
            a = acc[:, h * HEAD_DIM:(h + 1) * HEAD_DIM]
            inv = lax.rsqrt(jnp.mean(a * a, axis=-1, keepdims=True) + EPS)
            a = a * inv * gain
            rot = pltpu.roll(a, HEAD_DIM // 2, axis=1)
            heads.append(a * cos + rot * sin)
        return heads

    for which in range(3):
        for g, (out_ref, (_, dilation)) in enumerate(zip((qkv0_ref, qkv1_ref, qkv2_ref), ATTN_GROUPS)):
            @pl.when(j == q_lo + which * n_grp + g)
            def _(which=which, out_ref=out_ref, dilation=dilation):
                if which == 2:
                    heads = [acc[:, h * HEAD_DIM:(h + 1) * HEAD_DIM] for h in range(HEADS_PER_GROUP)]
                else:
                    heads = normed_rope(qg_ref[...] if which == 0 else kg_ref[...])
                for h, a in enumerate(heads):
                    cols = slice(h * HEAD_DIM, (h + 1) * HEAD_DIM)
                    if dilation == 1:
                        out_ref[which, 0, :, cols] = a.astype(BF16)
                    else:
                        de_ref[h] = a
                        for r in range(dilation):
                            rows = de_ref[h, pl.ds(r, tm // dilation, stride=dilation), :]
                            out_ref[which, r, :, cols] = rows.astype(BF16)


def _in_proj(x2, norm_g, w_in_b, q_norm_g, k_norm_g, cos2, sin2, batch, seq, q_lo):
    t, d = x2.shape
    width = w_in_b.shape[1]
    tm = _tile(seq, 1024)
    pos_tiles = seq // tm
    n_tiles = width // COL_TILE
    n_gate = (2 * d) // COL_TILE
    grid = (t // tm, n_tiles)
    qkv_shapes, qkv_specs = [], []
    for _, dilation in ATTN_GROUPS:
        assert tm % (dilation * BF16_SUBLANES) == 0
        qkv_shapes.append(jax.ShapeDtypeStruct((3, batch, dilation, seq // dilation, GROUP_WIDTH), BF16))
        qkv_specs.append(pl.BlockSpec((3, None, dilation, tm // dilation, GROUP_WIDTH),
                                      lambda i, j: (0, i // pos_tiles, 0, i % pos_tiles, 0)))
    return pl.pallas_call(
        functools.partial(_in_proj_kernel, q_lo=q_lo),
        out_shape=[jax.ShapeDtypeStruct((t, q_lo * COL_TILE), BF16)] + qkv_shapes,
        grid=grid,
        in_specs=[
            pl.BlockSpec((tm, d), lambda i, j: (i, 0)),
            pl.BlockSpec((1, d), lambda i, j: (0, 0)),
            pl.BlockSpec((d, COL_TILE), lambda i, j: (0, (j + n_tiles - n_gate) % n_tiles)),
            pl.BlockSpec((1, HEAD_DIM), lambda i, j: (0, 0)),
            pl.BlockSpec((1, HEAD_DIM), lambda i, j: (0, 0)),
            pl.BlockSpec((tm, HEAD_DIM), lambda i, j: (i % pos_tiles, 0)),
            pl.BlockSpec((tm, HEAD_DIM), lambda i, j: (i % pos_tiles, 0)),
        ],
        out_specs=[pl.BlockSpec((tm, COL_TILE), lambda i, j: (i, jnp.minimum(j, q_lo - 1)))] + qkv_specs,
        scratch_shapes=[pltpu.VMEM((tm, d), BF16), pltpu.VMEM((HEADS_PER_GROUP, tm, HEAD_DIM), F32)],
        compiler_params=_params(("parallel", "arbitrary")),
        name="in_proj",
    )(x2, norm_g, w_in_b, q_norm_g, k_norm_g, cos2, sin2)


def _attn_kernel(q_ref, kp_ref, km_ref, kn_ref, vp_ref, vm_ref, vn_ref, o_ref, lse_ref,
                 kx_ref, vx_ref, *, tq, sub, radius, length):
    i = pl.program_id(2)
    kx_ref[0:radius, :] = kp_ref[...]
    kx_ref[radius:radius + tq, :] = km_ref[...]
    kx_ref[radius + tq:, :] = kn_ref[...]
    vx_ref[0:radius, :] = vp_ref[...]
    vx_ref[radius:radius + tq, :] = vm_ref[...]
    vx_ref[radius + tq:, :] = vn_ref[...]
    nk = sub + 2 * radius
    scale = HEAD_DIM ** -0.5
    rel = lax.broadcasted_iota(jnp.int32, (sub, nk), 1) - radius - lax.broadcasted_iota(jnp.int32, (sub, nk), 0)
    band = jnp.abs(rel) <= radius
    for c in range(tq // sub):
        kpos = i * tq + c * sub - radius + lax.broadcasted_iota(jnp.int32, (sub, nk), 1)
        valid = band & (kpos >= 0) & (kpos < length)
        for h in range(HEADS_PER_GROUP):
            cols = slice(h * HEAD_DIM, (h + 1) * HEAD_DIM)
            q = q_ref[c * sub:(c + 1) * sub, cols]
            k = kx_ref[c * sub:c * sub + nk, cols]
            v = vx_ref[c * sub:c * sub + nk, cols]
            s = lax.dot_general(q, k, (((1,), (1,)), ((), ())), preferred_element_type=F32) * scale
            s = jnp.where(valid, s, NEG_INF)
            m = jnp.max(s, axis=-1, keepdims=True)
            p = jnp.exp(s - m)
            denom = jnp.sum(p, axis=-1, keepdims=True)
            o = jnp.dot(p.astype(BF16), v, preferred_element_type=F32) / denom
            o_ref[c * sub:(c + 1) * sub, cols] = o.astype(o_ref.dtype)
            lse = m + jnp.log(denom)
            lse_ref[c * sub:(c + 1) * sub, cols] = jnp.broadcast_to(lse, (sub, HEAD_DIM))


def _attention_group(qkv, group, window, dilation):
    _, batch, _, length, _ = qkv.shape
    radius = window // (2 * dilation)
    tq = _tile(length, 512)
    sub = min(tq, 2 * radius)
    assert tq % sub == 0 and tq % radius == 0 and length % radius == 0
    halo_per_tile = tq // radius
    n_halo = length // radius

    def main(which):
        return pl.BlockSpec((None, None, None, tq, GROUP_WIDTH), lambda b, r, i: (which, b, r, i, 0))

    def prev(which):
        return pl.BlockSpec((None, None, None, radius, GROUP_WIDTH),
                            lambda b, r, i: (which, b, r, jnp.maximum(i * halo_per_tile - 1, 0), 0))

    def nxt(which):
        return pl.BlockSpec((None, None, None, radius, GROUP_WIDTH),
                            lambda b, r, i: (which, b, r, jnp.minimum((i + 1) * halo_per_tile, n_halo - 1), 0))

    out_spec = pl.BlockSpec((None, None, tq, GROUP_WIDTH), lambda b, r, i: (b, r, i, 0))
    out_shape = (batch, dilation, length, GROUP_WIDTH)
    return pl.pallas_call(
        functools.partial(_attn_kernel, tq=tq, sub=sub, radius=radius, length=length),
        out_shape=(jax.ShapeDtypeStruct(out_shape, BF16), jax.ShapeDtypeStruct(out_shape, F32)),
        grid=(batch, dilation, length // tq),
        in_specs=[main(0), prev(1), main(1), nxt(1), prev(2), main(2), nxt(2)],
        out_specs=(out_spec, out_spec),
        scratch_shapes=[pltpu.VMEM((tq + 2 * radius, GROUP_WIDTH), BF16),
                        pltpu.VMEM((tq + 2 * radius, GROUP_WIDTH), BF16)],
        compiler_params=_params(("parallel", "parallel", "arbitrary")),
        name=f"attn_g{group}",
    )(qkv, qkv, qkv, qkv, qkv, qkv, qkv)


def _merge_kernel(x_ref, g_ref, up_ref, um_ref, un_ref,
                  o0_ref, o1_ref, o2_ref, l0_ref, l1_ref, l2_ref,
                  bg_ref, wp_ref, ps_ref, wpo_ref, wao_ref, wo_ref, fg_ref,
                  x1_ref, h2_ref, ux_ref, oi_ref, li_ref, *, tm, seq):
    i = pl.program_id(0)
    d = x_ref.shape[1]
    pool_w = um_ref.shape[1]
    n_pool = len(POOL_WINDOWS)
    gdim = pool_w // n_pool

    pos0 = (i * tm) % seq
    pos_p = pos0 - HALO + lax.broadcasted_iota(jnp.int32, (HALO, 1), 0)
    pos_n = pos0 + tm + lax.broadcasted_iota(jnp.int32, (HALO, 1), 0)
    ux_ref[0:HALO, :] = jnp.where(pos_p >= 0, up_ref[...].astype(F32), 0.0)
    ux_ref[HALO:HALO + tm, :] = um_ref[...].astype(F32)
    ux_ref[HALO + tm:, :] = jnp.where(pos_n < seq, un_ref[...].astype(F32), 0.0)

    pos = pos0 + lax.broadcasted_iota(jnp.int32, (tm, 1), 0)
    mixed = []
    for g, window in enumerate(POOL_WINDOWS):
        rad = window // 2
        cols = slice(g * gdim, (g + 1) * gdim)
        u = ux_ref[HALO:HALO + tm, cols]
        tot = u
        for t in range(1, rad + 1):
            tot = tot + ux_ref[HALO - t:HALO - t + tm, cols] + ux_ref[HALO + t:HALO + t + tm, cols]
        count = (jnp.minimum(pos + rad + 1, seq) - jnp.maximum(pos - rad, 0)).astype(F32)
        centred = tot / count - u
        mg = jnp.dot(centred.astype(BF16), wp_ref[g], preferred_element_type=F32)
        mixed.append((mg * ps_ref[:, cols]).astype(BF16))
    pool_d = jnp.dot(jnp.concatenate(mixed, axis=1), wpo_ref[...], preferred_element_type=F32)

    outs, lses = [], []
    for g, (o_ref, l_ref) in enumerate(((o0_ref, l0_ref), (o1_ref, l1_ref), (o2_ref, l2_ref))):
        dilation = o_ref.shape[0]
        if dilation == 1:
            outs.append(o_ref[0].astype(F32))
            lses.append(l_ref[0])
        else:
            for h in range(HEADS_PER_GROUP):
                cols = slice(h * HEAD_DIM, (h + 1) * HEAD_DIM)
                for r in range(dilation):
                    rows = pl.ds(r, tm // dilation, stride=dilation)
                    oi_ref[g, h, rows, :] = o_ref[r, :, cols].astype(F32)
                    li_ref[g, h, rows, :] = l_ref[r, :, cols]
            outs.append(jnp.concatenate([oi_ref[g, h] for h in range(HEADS_PER_GROUP)], axis=1))
            lses.append(jnp.concatenate([li_ref[g, h] for h in range(HEADS_PER_GROUP)], axis=1))
    l0, l1, l2 = lses
    lmax = jnp.maximum(jnp.maximum(l0, l1), l2)
    e0, e1, e2 = jnp.exp(l0 - lmax), jnp.exp(l1 - lmax), jnp.exp(l2 - lmax)
    attn = (e0 * outs[0] + e1 * outs[1] + e2 * outs[2]) / (e0 + e1 + e2)
    attn_d = jnp.dot(attn.astype(BF16), wao_ref[...], preferred_element_type=F32)

    gates = jax.nn.sigmoid(g_ref[...].astype(F32) + bg_ref[...])
    merged = gates[:, :d] * pool_d + gates[:, d:] * attn_d
    x1 = x_ref[...] + jnp.dot(merged.astype(BF16), wo_ref[...], preferred_element_type=F32)
    x1_ref[...] = x1
    inv = lax.rsqrt(jnp.mean(x1 * x1, axis=-1, keepdims=True) + EPS)
    h2_ref[...] = (x1 * inv * fg_ref[...]).astype(BF16)


def _merge(x2, z, attn_outs, b_gate, w_pool_b, pool_scale, w_pool_out_b, w_attn_out_b, w_o_b,
           ffn_norm_g, seq):
    t, d = x2.shape
    pool_w = w_pool_out_b.shape[0]
    tm = _tile(seq, 256)
    halo_per_tile = tm // HALO
    n_halo = t // HALO
    u_col = (2 * d) // pool_w

    row = lambda i: (i, 0)
    pos_tiles = seq // tm
    (o0, l0), (o1, l1), (o2, l2) = attn_outs
    grp = []
    for _, dilation in ATTN_GROUPS:
        assert tm % (dilation * BF16_SUBLANES) == 0
        grp.append(pl.BlockSpec((None, dilation, tm // dilation, GROUP_WIDTH),
                                lambda i: (i // pos_tiles, 0, i % pos_tiles, 0)))
    return pl.pallas_call(
        functools.partial(_merge_kernel, tm=tm, seq=seq),
        out_shape=(jax.ShapeDtypeStruct((t, d), F32), jax.ShapeDtypeStruct((t, d), BF16)),
        grid=(t // tm,),
        in_specs=[
            pl.BlockSpec((tm, d), row),
            pl.BlockSpec((tm, 2 * d), row),
            pl.BlockSpec((HALO, pool_w), lambda i: (jnp.maximum(i * halo_per_tile - 1, 0), u_col)),
            pl.BlockSpec((tm, pool_w), lambda i: (i, u_col)),
            pl.BlockSpec((HALO, pool_w), lambda i: (jnp.minimum((i + 1) * halo_per_tile, n_halo - 1), u_col)),
            grp[0], grp[1], grp[2], grp[0], grp[1], grp[2],
            _const_spec(b_gate.shape),
            _const_spec(w_pool_b.shape),
            _const_spec(pool_scale.shape),
            _const_spec(w_pool_out_b.shape),
            _const_spec(w_attn_out_b.shape),
            _const_spec(w_o_b.shape),
            _const_spec(ffn_norm_g.shape),
        ],
        out_specs=(pl.BlockSpec((tm, d), row), pl.BlockSpec((tm, d), row)),
        scratch_shapes=[pltpu.VMEM((tm + 2 * HALO, pool_w), F32),
                        pltpu.VMEM((len(ATTN_GROUPS), HEADS_PER_GROUP, tm, HEAD_DIM), F32),
                        pltpu.VMEM((len(ATTN_GROUPS), HEADS_PER_GROUP, tm, HEAD_DIM), F32)],
        compiler_params=_params(("parallel",)),
        name="merge",
    )(x2, z, z, z, z, o0, o1, o2, l0, l1, l2,
      b_gate, w_pool_b, pool_scale, w_pool_out_b, w_attn_out_b, w_o_b, ffn_norm_g)


def _ffn_kernel(hp_ref, hm_ref, hn_ref, x1_ref, wg_ref, wv_ref, cw_ref, cb_ref, wd_ref,
                y_ref, hx_ref, *, tm, seq):
    i = pl.program_id(0)
    f = pl.program_id(1)

    @pl.when(f == 0)
    def _():
        pos0 = (i * tm) % seq
        pos_p = pos0 - HALO + lax.broadcasted_iota(jnp.int32, (HALO, 1), 0)
        pos_n = pos0 + tm + lax.broadcasted_iota(jnp.int32, (HALO, 1), 0)
        hx_ref[0:HALO, :] = jnp.where(pos_p >= 0, hp_ref[...], jnp.zeros_like(hp_ref[...]))
        hx_ref[HALO:HALO + tm, :] = hm_ref[...]
        hx_ref[HALO + tm:, :] = jnp.where(pos_n < seq, hn_ref[...], jnp.zeros_like(hn_ref[...]))
        y_ref[...] = x1_ref[...]

    gate = jnp.dot(hx_ref[...], wg_ref[...], preferred_element_type=F32)
    val = jnp.dot(hm_ref[...], wv_ref[...], preferred_element_type=F32)
    half = CONV_WIDTH // 2
    conv = cb_ref[...]
    for tap in range(CONV_WIDTH):
        lo = HALO - half + tap
        conv = conv + gate[lo:lo + tm, :] * cw_ref[tap:tap + 1, :]
    act = jax.nn.gelu(conv) * val
    y_ref[...] += jnp.dot(act.astype(BF16), wd_ref[...], preferred_element_type=F32)


def _ffn(h2, x1, w_up_b, conv_w, conv_b, w_down_b, seq):
    t, d = x1.shape
    d_ff = w_down_b.shape[0]
    tm = _tile(seq, 512)
    tf = 512 if d_ff % 512 == 0 else 256
    nf = d_ff // tf
    halo_per_tile = tm // HALO
    n_halo = t // HALO
    return pl.pallas_call(
        functools.partial(_ffn_kernel, tm=tm, seq=seq),
        out_shape=jax.ShapeDtypeStruct((t, d), F32),
        grid=(t // tm, nf),
        in_specs=[
            pl.BlockSpec((HALO, d), lambda i, f: (jnp.maximum(i * halo_per_tile - 1, 0), 0)),
            pl.BlockSpec((tm, d), lambda i, f: (i, 0)),
            pl.BlockSpec((HALO, d), lambda i, f: (jnp.minimum((i + 1) * halo_per_tile, n_halo - 1), 0)),
            pl.BlockSpec((tm, d), lambda i, f: (i, 0)),
            pl.BlockSpec((d, tf), lambda i, f: (0, f)),
            pl.BlockSpec((d, tf), lambda i, f: (0, nf + f)),
            pl.BlockSpec((CONV_WIDTH, tf), lambda i, f: (0, f)),
            pl.BlockSpec((1, tf), lambda i, f: (0, f)),
            pl.BlockSpec((tf, d), lambda i, f: (f, 0)),
        ],
        out_specs=pl.BlockSpec((tm, d), lambda i, f: (i, 0)),
        scratch_shapes=[pltpu.VMEM((tm + 2 * HALO, d), BF16)],
        compiler_params=_params(("parallel", "arbitrary")),
        name="ffn",
    )(h2, h2, h2, x1, w_up_b, w_up_b, conv_w, conv_b, w_down_b)


def _rope_tables(seq):
    half = HEAD_DIM // 2
    inv_freq = ROPE_THETA ** (-jnp.arange(half, dtype=F32) / half)
    ang = jnp.arange(seq, dtype=F32)[:, None] * inv_freq[None, :]
    cos, sin = jnp.cos(ang), jnp.sin(ang)
    return jnp.concatenate([cos, cos], axis=-1), jnp.concatenate([-sin, sin], axis=-1)


def _layer(x, p):
    batch, seq, d = x.shape
    x2 = x.reshape(batch * seq, d)
    cos2, sin2 = _rope_tables(seq)
    q_lo = (2 * d + p["pool_w"]) // COL_TILE
    z, *qkv = _in_proj(x2, p["mix_norm_g"], p["w_in"], p["q_norm_g"], p["k_norm_g"], cos2, sin2,
                       batch, seq, q_lo)
    attn_outs = [_attention_group(qkv[g], g, window, dilation)
                 for g, (window, dilation) in enumerate(ATTN_GROUPS)]
    x1, h2 = _merge(x2, z, attn_outs, p["b_gate"], p["w_pool"], p["pool_scale"], p["w_pool_out"],
                    p["w_attn_out"], p["w_o"], p["ffn_norm_g"], seq)
    y = _ffn(h2, x1, p["w_up"], p["conv_w"], p["conv_b"], p["w_down"], seq)
    return y.reshape(batch, seq, d)


def kernel(x_prompt, x_sample, mix_norm_g, w_in, b_gate, q_norm_g, k_norm_g, w_pool, pool_scale,
           w_pool_out, w_attn_out, w_o, ffn_norm_g, w_up, conv_w, conv_b, w_down):
    depth = w_in.shape[0]
    pool_w = w_pool_out.shape[1]
    layers = []
    for l in range(depth):
        layers.append(dict(
            pool_w=pool_w,
            mix_norm_g=mix_norm_g[l][None, :], w_in=w_in[l].astype(BF16), b_gate=b_gate[l][None, :],
            q_norm_g=q_norm_g[l][None, :], k_norm_g=k_norm_g[l][None, :],
            w_pool=w_pool[l].astype(BF16), pool_scale=pool_scale[l][None, :],
            w_pool_out=w_pool_out[l].astype(BF16), w_attn_out=w_attn_out[l].astype(BF16),
            w_o=w_o[l].astype(BF16), ffn_norm_g=ffn_norm_g[l][None, :],
            w_up=w_up[l].astype(BF16), conv_w=conv_w[l], conv_b=conv_b[l][None, :],
            w_down=w_down[l].astype(BF16)))

    def trunk(x):
        for p in layers:
            x = _layer(x, p)
        return x

    return (trunk(x_prompt), trunk(x_sample))
```

```python
import functools

import jax
import jax.numpy as jnp
from jax import lax
from jax.experimental import pallas as pl
from jax.experimental.pallas import tpu as pltpu

F32 = jnp.float32
BF16 = jnp.bfloat16

POOL_WINDOWS = (2, 4, 8, 16)
HEAD_DIM = 128
ATTN_GROUPS = ((128, 1), (512, 4), (2048, 16))
HEADS_PER_GROUP = 4
GROUP_WIDTH = HEADS_PER_GROUP * HEAD_DIM
ATTN_WIDTH = len(ATTN_GROUPS) * GROUP_WIDTH
ROPE_THETA = 10000.0
CONV_WIDTH = 3
EPS = 1e-6
NEG_INF = -1e30

LANES = 128
BF16_SUBLANES = 16
V7X_VMEM_BYTES = 64 * 1024 * 1024
VMEM_LIMIT_BYTES = V7X_VMEM_BYTES - 8 * 1024 * 1024

COL_TILE = GROUP_WIDTH
HALO = BF16_SUBLANES


def _tile(n, target):
    t = min(n, target)
    while n % t or t % BF16_SUBLANES:
        t -= 1
    return t


def _const_spec(shape):
    nd = len(shape)
    return pl.BlockSpec(shape, lambda *_: (0,) * nd, pipeline_mode=pl.Buffered(1))


def _params(semantics):
    return pltpu.CompilerParams(dimension_semantics=semantics, vmem_limit_bytes=VMEM_LIMIT_BYTES)


def _in_proj_kernel(x_ref, ng_ref, w_ref, qg_ref, kg_ref, cos_ref, sin_ref,
                    zr_ref, qkv0_ref, qkv1_ref, qkv2_ref, h_ref, acc0_ref, acc1_ref, de_ref,
                    *, q_lo, n_tiles):
    s = pl.program_id(1)
    tm = x_ref.shape[0]
    n_grp = len(ATTN_GROUPS)
    accs = (acc0_ref, acc1_ref)

    def project(tile):
        res = jnp.dot(h_ref[...], w_ref[...], preferred_element_type=F32)
        for h in range(HEADS_PER_GROUP):
            accs[tile % 2][h] = res[:, h * HEAD_DIM:(h + 1) * HEAD_DIM]

    @pl.when(s == 0)
    def _():
        x = x_ref[...]
        inv = lax.rsqrt(jnp.mean(x * x, axis=-1, keepdims=True) + EPS)
        h_ref[...] = (x * inv * ng_ref[...]).astype(BF16)
        project(0)

    for parity in range(2):
        @pl.when((s >= 1) & (s <= q_lo) & ((s - 1) % 2 == parity))
        def _(parity=parity):
            project(parity + 1)
            for h in range(HEADS_PER_GROUP):
                zr_ref[:, h * HEAD_DIM:(h + 1) * HEAD_DIM] = accs[parity][h].astype(BF16)

    def normed_rope(a, gain):
        inv = lax.rsqrt(jnp.mean(a * a, axis=-1, keepdims=True) + EPS)
        a = a * inv * gain
        rot = pltpu.roll(a, HEAD_DIM // 2, axis=1)
        return a * cos_ref[...] + rot * sin_ref[...]

    for which in range(3):
        for g, (out_ref, (_, dilation)) in enumerate(zip((qkv0_ref, qkv1_ref, qkv2_ref), ATTN_GROUPS)):
            tile = q_lo + which * n_grp + g

            @pl.when(s == tile + 1)
            def _(which=which, out_ref=out_ref, dilation=dilation, tile=tile):
                if tile + 1 < n_tiles:
                    project(tile + 1)
                acc_ref = accs[tile % 2]
                for h in range(HEADS_PER_GROUP):
                    cols = slice(h * HEAD_DIM, (h + 1) * HEAD_DIM)
                    src = acc_ref
                    if which < 2:
                        a = normed_rope(acc_ref[h], qg_ref[...] if which == 0 else kg_ref[...])
                        if dilation == 1:
                            out_ref[which, 0, :, cols] = a.astype(BF16)
                            continue
                        de_ref[h] = a
                        src = de_ref
                    for r in range(dilation):
                        rows = src[h, pl.ds(r, tm // dilation, stride=dilation), :]
                        out_ref[which, r, :, cols] = rows.astype(BF16)


def _in_proj(x2, norm_g, w_in_b, q_norm_g, k_norm_g, cos2, sin2, batch, seq, q_lo):
    t, d = x2.shape
    width = w_in_b.shape[1]
    tm = _tile(seq, 1024)
    pos_tiles = seq // tm
    n_tiles = width // COL_TILE
    n_gate = (2 * d) // COL_TILE
    grid = (t // tm, n_tiles + 1)
    qkv_shapes, qkv_specs = [], []
    for _, dilation in ATTN_GROUPS:
        assert tm % (dilation * BF16_SUBLANES) == 0
        qkv_shapes.append(jax.ShapeDtypeStruct((3, batch, dilation, seq // dilation, GROUP_WIDTH), BF16))
        qkv_specs.append(pl.BlockSpec((3, None, dilation, tm // dilation, GROUP_WIDTH),
                                      lambda i, j: (0, i // pos_tiles, 0, i % pos_tiles, 0)))
    return pl.pallas_call(
        functools.partial(_in_proj_kernel, q_lo=q_lo, n_tiles=n_tiles),
        out_shape=[jax.ShapeDtypeStruct((t, q_lo * COL_TILE), BF16)] + qkv_shapes,
        grid=grid,
        in_specs=[
            pl.BlockSpec((tm, d), lambda i, j: (i, 0)),
            pl.BlockSpec((1, d), lambda i, j: (0, 0)),
            pl.BlockSpec((d, COL_TILE),
                         lambda i, j: (0, (jnp.minimum(j, n_tiles - 1) + n_tiles - n_gate) % n_tiles)),
            pl.BlockSpec((1, HEAD_DIM), lambda i, j: (0, 0)),
            pl.BlockSpec((1, HEAD_DIM), lambda i, j: (0, 0)),
            pl.BlockSpec((tm, HEAD_DIM), lambda i, j: (i % pos_tiles, 0)),
            pl.BlockSpec((tm, HEAD_DIM), lambda i, j: (i % pos_tiles, 0)),
        ],
        out_specs=[pl.BlockSpec((tm, COL_TILE), lambda i, j: (i, jnp.clip(j - 1, 0, q_lo - 1)))] + qkv_specs,
        scratch_shapes=[pltpu.VMEM((tm, d), BF16)] + [pltpu.VMEM((HEADS_PER_GROUP, tm, HEAD_DIM), F32)] * 3,
        compiler_params=_params(("parallel", "arbitrary")),
        name="in_proj",
    )(x2, norm_g, w_in_b, q_norm_g, k_norm_g, cos2, sin2)


def _attn_kernel(q_ref, kp_ref, km_ref, kn_ref, vp_ref, vm_ref, vn_ref, o_ref, lse_ref,
                 kx_ref, vx_ref, *, tq, sub, radius, length):
    i = pl.program_id(2)
    kx_ref[0:radius, :] = kp_ref[...]
    kx_ref[radius:radius + tq, :] = km_ref[...]
    kx_ref[radius + tq:, :] = kn_ref[...]
    vx_ref[0:radius, :] = vp_ref[...]
    vx_ref[radius:radius + tq, :] = vm_ref[...]
    vx_ref[radius + tq:, :] = vn_ref[...]
    nk = sub + 2 * radius
    scale = HEAD_DIM ** -0.5
    rel = lax.broadcasted_iota(jnp.int32, (sub, nk), 1) - radius - lax.broadcasted_iota(jnp.int32, (sub, nk), 0)
    band = jnp.abs(rel) <= radius
    for c in range(tq // sub):
        kpos = i * tq + c * sub - radius + lax.broadcasted_iota(jnp.int32, (sub, nk), 1)
        valid = band & (kpos >= 0) & (kpos < length)
        for h in range(HEADS_PER_GROUP):
            cols = slice(h * HEAD_DIM, (h + 1) * HEAD_DIM)
            q = q_ref[c * sub:(c + 1) * sub, cols]
            k = kx_ref[c * sub:c * sub + nk, cols]
            v = vx_ref[c * sub:c * sub + nk, cols]
            s = lax.dot_general(q, k, (((1,), (1,)), ((), ())), preferred_element_type=F32) * scale
            s = jnp.where(valid, s, NEG_INF)
            m = jnp.max(s, axis=-1, keepdims=True)
            p = jnp.exp(s - m)
            denom = jnp.sum(p, axis=-1, keepdims=True)
            o = jnp.dot(p.astype(BF16), v, preferred_element_type=F32) / denom
            o_ref[c * sub:(c + 1) * sub, cols] = o.astype(o_ref.dtype)
            lse = m + jnp.log(denom)
            lse_ref[c * sub:(c + 1) * sub, cols] = jnp.broadcast_to(lse, (sub, HEAD_DIM))


def _attention_group(qkv, group, window, dilation):
    _, batch, _, length, _ = qkv.shape
    radius = window // (2 * dilation)
    tq = _tile(length, 512)
    sub = min(tq, 2 * radius)
    assert tq % sub == 0 and tq % radius == 0 and length % radius == 0
    halo_per_tile = tq // radius
    n_halo = length // radius

    def main(which):
        return pl.BlockSpec((None, None, None, tq, GROUP_WIDTH), lambda b, r, i: (which, b, r, i, 0))

    def prev(which):
        return pl.BlockSpec((None, None, None, radius, GROUP_WIDTH),
                            lambda b, r, i: (which, b, r, jnp.maximum(i * halo_per_tile - 1, 0), 0))

    def nxt(which):
        return pl.BlockSpec((None, None, None, radius, GROUP_WIDTH),
                            lambda b, r, i: (which, b, r, jnp.minimum((i + 1) * halo_per_tile, n_halo - 1), 0))

    out_spec = pl.BlockSpec((None, None, tq, GROUP_WIDTH), lambda b, r, i: (b, r, i, 0))
    out_shape = (batch, dilation, length, GROUP_WIDTH)
    return pl.pallas_call(
        functools.partial(_attn_kernel, tq=tq, sub=sub, radius=radius, length=length),
        out_shape=(jax.ShapeDtypeStruct(out_shape, BF16), jax.ShapeDtypeStruct(out_shape, F32)),
        grid=(batch, dilation, length // tq),
        in_specs=[main(0), prev(1), main(1), nxt(1), prev(2), main(2), nxt(2)],
        out_specs=(out_spec, out_spec),
        scratch_shapes=[pltpu.VMEM((tq + 2 * radius, GROUP_WIDTH), BF16),
                        pltpu.VMEM((tq + 2 * radius, GROUP_WIDTH), BF16)],
        compiler_params=_params(("parallel", "parallel", "arbitrary")),
        name=f"attn_g{group}",
    )(qkv, qkv, qkv, qkv, qkv, qkv, qkv)


def _merge_kernel(x_ref, g_ref, up_ref, um_ref, un_ref,
                  o0_ref, o1_ref, o2_ref, l0_ref, l1_ref, l2_ref,
                  bg_ref, wp_ref, ps_ref, wpo_ref, wao_ref, wo_ref, fg_ref,
                  x1_ref, h2_ref, ux_ref, oi_ref, li_ref, *, tm, seq):
    i = pl.program_id(0)
    d = x_ref.shape[1]
    pool_w = um_ref.shape[1]
    n_pool = len(POOL_WINDOWS)
    gdim = pool_w // n_pool

    pos0 = (i * tm) % seq
    pos_p = pos0 - HALO + lax.broadcasted_iota(jnp.int32, (HALO, 1), 0)
    pos_n = pos0 + tm + lax.broadcasted_iota(jnp.int32, (HALO, 1), 0)
    ux_ref[0:HALO, :] = jnp.where(pos_p >= 0, up_ref[...].astype(F32), 0.0)
    ux_ref[HALO:HALO + tm, :] = um_ref[...].astype(F32)
    ux_ref[HALO + tm:, :] = jnp.where(pos_n < seq, un_ref[...].astype(F32), 0.0)

    pos = pos0 + lax.broadcasted_iota(jnp.int32, (tm, 1), 0)
    mixed = []
    for g, window in enumerate(POOL_WINDOWS):
        rad = window // 2
        cols = slice(g * gdim, (g + 1) * gdim)
        u = ux_ref[HALO:HALO + tm, cols]
        tot = u
        for t in range(1, rad + 1):
            tot = tot + ux_ref[HALO - t:HALO - t + tm, cols] + ux_ref[HALO + t:HALO + t + tm, cols]
        count = (jnp.minimum(pos + rad + 1, seq) - jnp.maximum(pos - rad, 0)).astype(F32)
        centred = tot / count - u
        mg = jnp.dot(centred.astype(BF16), wp_ref[g], preferred_element_type=F32)
        mixed.append((mg * ps_ref[:, cols]).astype(BF16))
    pool_d = jnp.dot(jnp.concatenate(mixed, axis=1), wpo_ref[...], preferred_element_type=F32)

    outs, lses = [], []
    for g, (o_ref, l_ref) in enumerate(((o0_ref, l0_ref), (o1_ref, l1_ref), (o2_ref, l2_ref))):
        dilation = o_ref.shape[0]
        if dilation == 1:
            outs.append(o_ref[0].astype(F32))
            lses.append(l_ref[0])
        else:
            for h in range(HEADS_PER_GROUP):
                cols = slice(h * HEAD_DIM, (h + 1) * HEAD_DIM)
                for r in range(dilation):
                    rows = pl.ds(r, tm // dilation, stride=dilation)
                    oi_ref[g, h, rows, :] = o_ref[r, :, cols].astype(F32)
                    li_ref[g, h, rows, :] = l_ref[r, :, cols]
            outs.append(jnp.concatenate([oi_ref[g, h] for h in range(HEADS_PER_GROUP)], axis=1))
            lses.append(jnp.concatenate([li_ref[g, h] for h in range(HEADS_PER_GROUP)], axis=1))
    l0, l1, l2 = lses
    lmax = jnp.maximum(jnp.maximum(l0, l1), l2)
    e0, e1, e2 = jnp.exp(l0 - lmax), jnp.exp(l1 - lmax), jnp.exp(l2 - lmax)
    attn = (e0 * outs[0] + e1 * outs[1] + e2 * outs[2]) / (e0 + e1 + e2)
    attn_d = jnp.dot(attn.astype(BF16), wao_ref[...], preferred_element_type=F32)

    gates = jax.nn.sigmoid(g_ref[...].astype(F32) + bg_ref[...])
    merged = gates[:, :d] * pool_d + gates[:, d:] * attn_d
    x1 = x_ref[...] + jnp.dot(merged.astype(BF16), wo_ref[...], preferred_element_type=F32)
    x1_ref[...] = x1
    inv = lax.rsqrt(jnp.mean(x1 * x1, axis=-1, keepdims=True) + EPS)
    h2_ref[...] = (x1 * inv * fg_ref[...]).astype(BF16)


def _merge(x2, z, attn_outs, b_gate, w_pool_b, pool_scale, w_pool_out_b, w_attn_out_b, w_o_b,
           ffn_norm_g, seq):
    t, d = x2.shape
    pool_w = w_pool_out_b.shape[0]
    tm = _tile(seq, 256)
    halo_per_tile = tm // HALO
    n_halo = t // HALO
    u_col = (2 * d) // pool_w

    row = lambda i: (i, 0)
    pos_tiles = seq // tm
    (o0, l0), (o1, l1), (o2, l2) = attn_outs
    grp = []
    for _, dilation in ATTN_GROUPS:
        assert tm % (dilation * BF16_SUBLANES) == 0
        grp.append(pl.BlockSpec((None, dilation, tm // dilation, GROUP_WIDTH),
                                lambda i: (i // pos_tiles, 0, i % pos_tiles, 0)))
    return pl.pallas_call(
        functools.partial(_merge_kernel, tm=tm, seq=seq),
        out_shape=(jax.ShapeDtypeStruct((t, d), F32), jax.ShapeDtypeStruct((t, d), BF16)),
        grid=(t // tm,),
        in_specs=[
            pl.BlockSpec((tm, d), row),
            pl.BlockSpec((tm, 2 * d), row),
            pl.BlockSpec((HALO, pool_w), lambda i: (jnp.maximum(i * halo_per_tile - 1, 0), u_col)),
            pl.BlockSpec((tm, pool_w), lambda i: (i, u_col)),
            pl.BlockSpec((HALO, pool_w), lambda i: (jnp.minimum((i + 1) * halo_per_tile, n_halo - 1), u_col)),
            grp[0], grp[1], grp[2], grp[0], grp[1], grp[2],
            _const_spec(b_gate.shape),
            _const_spec(w_pool_b.shape),
            _const_spec(pool_scale.shape),
            _const_spec(w_pool_out_b.shape),
            _const_spec(w_attn_out_b.shape),
            _const_spec(w_o_b.shape),
            _const_spec(ffn_norm_g.shape),
        ],
        out_specs=(pl.BlockSpec((tm, d), row), pl.BlockSpec((tm, d), row)),
        scratch_shapes=[pltpu.VMEM((tm + 2 * HALO, pool_w), F32),
                        pltpu.VMEM((len(ATTN_GROUPS), HEADS_PER_GROUP, tm, HEAD_DIM), F32),
                        pltpu.VMEM((len(ATTN_GROUPS), HEADS_PER_GROUP, tm, HEAD_DIM), F32)],
        compiler_params=_params(("parallel",)),
        name="merge",
    )(x2, z, z, z, z, o0, o1, o2, l0, l1, l2,
      b_gate, w_pool_b, pool_scale, w_pool_out_b, w_attn_out_b, w_o_b, ffn_norm_g)


def _ffn_kernel(hp_ref, hm_ref, hn_ref, x1_ref, wg_ref, wv_ref, cw_ref, cb_ref, wd_ref,
                y_ref, hx_ref, *, tm, seq):
    i = pl.program_id(0)
    f = pl.program_id(1)

    @pl.when(f == 0)
    def _():
        pos0 = (i * tm) % seq
        pos_p = pos0 - HALO + lax.broadcasted_iota(jnp.int32, (HALO, 1), 0)
        pos_n = pos0 + tm + lax.broadcasted_iota(jnp.int32, (HALO, 1), 0)
        hx_ref[0:HALO, :] = jnp.where(pos_p >= 0, hp_ref[...], jnp.zeros_like(hp_ref[...]))
        hx_ref[HALO:HALO + tm, :] = hm_ref[...]
        hx_ref[HALO + tm:, :] = jnp.where(pos_n < seq, hn_ref[...], jnp.zeros_like(hn_ref[...]))
        y_ref[...] = x1_ref[...]

    gate = jnp.dot(hx_ref[...], wg_ref[...], preferred_element_type=F32)
    val = jnp.dot(hm_ref[...], wv_ref[...], preferred_element_type=F32)
    half = CONV_WIDTH // 2
    conv = cb_ref[...]
    for tap in range(CONV_WIDTH):
        lo = HALO - half + tap
        conv = conv + gate[lo:lo + tm, :] * cw_ref[tap:tap + 1, :]
    act = jax.nn.gelu(conv) * val
    y_ref[...] += jnp.dot(act.astype(BF16), wd_ref[...], preferred_element_type=F32)


def _ffn(h2, x1, w_up_b, conv_w, conv_b, w_down_b, seq):
    t, d = x1.shape
    d_ff = w_down_b.shape[0]
    tm = _tile(seq, 512)
    tf = 512 if d_ff % 512 == 0 else 256
    nf = d_ff // tf
    halo_per_tile = tm // HALO
    n_halo = t // HALO
    return pl.pallas_call(
        functools.partial(_ffn_kernel, tm=tm, seq=seq),
        out_shape=jax.ShapeDtypeStruct((t, d), F32),
        grid=(t // tm, nf),
        in_specs=[
            pl.BlockSpec((HALO, d), lambda i, f: (jnp.maximum(i * halo_per_tile - 1, 0), 0)),
            pl.BlockSpec((tm, d), lambda i, f: (i, 0)),
            pl.BlockSpec((HALO, d), lambda i, f: (jnp.minimum((i + 1) * halo_per_tile, n_halo - 1), 0)),
            pl.BlockSpec((tm, d), lambda i, f: (i, 0)),
            pl.BlockSpec((d, tf), lambda i, f: (0, f)),
            pl.BlockSpec((d, tf), lambda i, f: (0, nf + f)),
            pl.BlockSpec((CONV_WIDTH, tf), lambda i, f: (0, f)),
            pl.BlockSpec((1, tf), lambda i, f: (0, f)),
            pl.BlockSpec((tf, d), lambda i, f: (f, 0)),
        ],
        out_specs=pl.BlockSpec((tm, d), lambda i, f: (i, 0)),
        scratch_shapes=[pltpu.VMEM((tm + 2 * HALO, d), BF16)],
        compiler_params=_params(("parallel", "arbitrary")),
        name="ffn",
    )(h2, h2, h2, x1, w_up_b, w_up_b, conv_w, conv_b, w_down_b)


def _rope_tables(seq):
    half = HEAD_DIM // 2
    inv_freq = ROPE_THETA ** (-jnp.arange(half, dtype=F32) / half)
    ang = jnp.arange(seq, dtype=F32)[:, None] * inv_freq[None, :]
    cos, sin = jnp.cos(ang), jnp.sin(ang)
    return jnp.concatenate([cos, cos], axis=-1), jnp.concatenate([-sin, sin], axis=-1)


def _layer(x, p):
    batch, seq, d = x.shape
    x2 = x.reshape(batch * seq, d)
    cos2, sin2 = _rope_tables(seq)
    q_lo = (2 * d + p["pool_w"]) // COL_TILE
    z, *qkv = _in_proj(x2, p["mix_norm_g"], p["w_in"], p["q_norm_g"], p["k_norm_g"], cos2, sin2,
                       batch, seq, q_lo)
    attn_outs = [_attention_group(qkv[g], g, window, dilation)
                 for g, (window, dilation) in enumerate(ATTN_GROUPS)]
    x1, h2 = _merge(x2, z, attn_outs, p["b_gate"], p["w_pool"], p["pool_scale"], p["w_pool_out"],
                    p["w_attn_out"], p["w_o"], p["ffn_norm_g"], seq)
    y = _ffn(h2, x1, p["w_up"], p["conv_w"], p["conv_b"], p["w_down"], seq)
    return y.reshape(batch, seq, d)


def kernel(x_prompt, x_sample, mix_norm_g, w_in, b_gate, q_norm_g, k_norm_g, w_pool, pool_scale,
           w_pool_out, w_attn_out, w_o, ffn_norm_g, w_up, conv_w, conv_b, w_down):
    depth = w_in.shape[0]
    pool_w = w_pool_out.shape[1]
    layers = []
    for l in range(depth):
        layers.append(dict(
            pool_w=pool_w,
            mix_norm_g=mix_norm_g[l][None, :], w_in=w_in[l].astype(BF16), b_gate=b_gate[l][None, :],
            q_norm_g=q_norm_g[l][None, :], k_norm_g=k_norm_g[l][None, :],
            w_pool=w_pool[l].astype(BF16), pool_scale=pool_scale[l][None, :],
            w_pool_out=w_pool_out[l].astype(BF16), w_attn_out=w_attn_out[l].astype(BF16),
            w_o=w_o[l].astype(BF16), ffn_norm_g=ffn_norm_g[l][None, :],
            w_up=w_up[l].astype(BF16), conv_w=conv_w[l], conv_b=conv_b[l][None, :],
            w_down=w_down[l].astype(BF16)))

    def trunk(x):
        for p in layers:
            x = _layer(x, p)
        return x

    return (trunk(x_prompt), trunk(x_sample))
```

```python
import functools

import jax
import jax.numpy as jnp
from jax import lax
from jax.experimental import pallas as pl
from jax.experimental.pallas import tpu as pltpu

F32 = jnp.float32
BF16 = jnp.bfloat16

POOL_WINDOWS = (2, 4, 8, 16)
HEAD_DIM = 128
ATTN_GROUPS = ((128, 1), (512, 4), (2048, 16))
HEADS_PER_GROUP = 4
GROUP_WIDTH = HEADS_PER_GROUP * HEAD_DIM
ATTN_WIDTH = len(ATTN_GROUPS) * GROUP_WIDTH
ROPE_THETA = 10000.0
CONV_WIDTH = 3
EPS = 1e-6
NEG_INF = -1e30

LANES = 128
BF16_SUBLANES = 16
V7X_VMEM_BYTES = 64 * 1024 * 1024
VMEM_LIMIT_BYTES = V7X_VMEM_BYTES - 8 * 1024 * 1024

COL_TILE = GROUP_WIDTH
HALO = BF16_SUBLANES


def _tile(n, target):
    t = min(n, target)
    while n % t or t % BF16_SUBLANES:
        t -= 1
    return t


def _const_spec(shape):
    nd = len(shape)
    return pl.BlockSpec(shape, lambda *_: (0,) * nd, pipeline_mode=pl.Buffered(1))


def _params(semantics):
    return pltpu.CompilerParams(dimension_semantics=semantics, vmem_limit_bytes=VMEM_LIMIT_BYTES)


def _in_proj_kernel(x_ref, ng_ref, w_ref, qg_ref, kg_ref, cos_ref, sin_ref,
                    zr_ref, qkv0_ref, qkv1_ref, qkv2_ref, h_ref, acc0_ref, acc1_ref, de_ref,
                    *, q_lo, n_tiles):
    s = pl.program_id(1)
    tm = x_ref.shape[0]
    n_grp = len(ATTN_GROUPS)
    accs = (acc0_ref, acc1_ref)

    def project(tile):
        res = jnp.dot(h_ref[...], w_ref[...], preferred_element_type=F32)
        for h in range(HEADS_PER_GROUP):
            accs[tile % 2][h] = res[:, h * HEAD_DIM:(h + 1) * HEAD_DIM]

    @pl.when(s == 0)
    def _():
        x = x_ref[...]
        inv = lax.rsqrt(jnp.mean(x * x, axis=-1, keepdims=True) + EPS)
        h_ref[...] = (x * inv * ng_ref[...]).astype(BF16)
        project(0)

    for parity in range(2):
        @pl.when((s >= 1) & (s <= q_lo) & ((s - 1) % 2 == parity))
        def _(parity=parity):
            project(parity + 1)
            for h in range(HEADS_PER_GROUP):
                zr_ref[:, h * HEAD_DIM:(h + 1) * HEAD_DIM] = accs[parity][h].astype(BF16)

    def normed_rope(a, gain):
        inv = lax.rsqrt(jnp.mean(a * a, axis=-1, keepdims=True) + EPS)
        a = a * inv * gain
        rot = pltpu.roll(a, HEAD_DIM // 2, axis=1)
        return a * cos_ref[...] + rot * sin_ref[...]

    for which in range(3):
        for g, (out_ref, (_, dilation)) in enumerate(zip((qkv0_ref, qkv1_ref, qkv2_ref), ATTN_GROUPS)):
            tile = q_lo + which * n_grp + g

            @pl.when(s == tile + 1)
            def _(which=which, out_ref=out_ref, dilation=dilation, tile=tile):
                if tile + 1 < n_tiles:
                    project(tile + 1)
                acc_ref = accs[tile % 2]
                for h in range(HEADS_PER_GROUP):
                    cols = slice(h * HEAD_DIM, (h + 1) * HEAD_DIM)
                    src = acc_ref
                    if which < 2:
                        a = normed_rope(acc_ref[h], qg_ref[...] if which == 0 else kg_ref[...])
                        if dilation == 1:
                            out_ref[which, 0, :, cols] = a.astype(BF16)
                            continue
                        de_ref[h] = a
                        src = de_ref
                    for r in range(dilation):
                        rows = src[h, pl.ds(r, tm // dilation, stride=dilation), :]
                        out_ref[which, r, :, cols] = rows.astype(BF16)


def _in_proj(x2, norm_g, w_in_b, q_norm_g, k_norm_g, cos2, sin2, batch, seq, q_lo):
    t, d = x2.shape
    n_tiles = w_in_b.shape[0]
    tm = _tile(seq, 1024)
    pos_tiles = seq // tm
    n_gate = (2 * d) // COL_TILE
    grid = (t // tm, n_tiles + 1)
    qkv_shapes, qkv_specs = [], []
    for _, dilation in ATTN_GROUPS:
        assert tm % (dilation * BF16_SUBLANES) == 0
        qkv_shapes.append(jax.ShapeDtypeStruct((3, batch, dilation, seq // dilation, GROUP_WIDTH), BF16))
        qkv_specs.append(pl.BlockSpec((3, None, dilation, tm // dilation, GROUP_WIDTH),
                                      lambda i, j: (0, i // pos_tiles, 0, i % pos_tiles, 0)))
    return pl.pallas_call(
        functools.partial(_in_proj_kernel, q_lo=q_lo, n_tiles=n_tiles),
        out_shape=[jax.ShapeDtypeStruct((t, q_lo * COL_TILE), BF16)] + qkv_shapes,
        grid=grid,
        in_specs=[
            pl.BlockSpec((tm, d), lambda i, j: (i, 0)),
            pl.BlockSpec((1, d), lambda i, j: (0, 0)),
            pl.BlockSpec((None, d, COL_TILE),
                         lambda i, j: ((jnp.minimum(j, n_tiles - 1) + n_tiles - n_gate) % n_tiles, 0, 0)),
            pl.BlockSpec((1, HEAD_DIM), lambda i, j: (0, 0)),
            pl.BlockSpec((1, HEAD_DIM), lambda i, j: (0, 0)),
            pl.BlockSpec((tm, HEAD_DIM), lambda i, j: (i % pos_tiles, 0)),
            pl.BlockSpec((tm, HEAD_DIM), lambda i, j: (i % pos_tiles, 0)),
        ],
        out_specs=[pl.BlockSpec((tm, COL_TILE), lambda i, j: (i, jnp.clip(j - 1, 0, q_lo - 1)))] + qkv_specs,
        scratch_shapes=[pltpu.VMEM((tm, d), BF16)] + [pltpu.VMEM((HEADS_PER_GROUP, tm, HEAD_DIM), F32)] * 3,
        compiler_params=_params(("parallel", "arbitrary")),
        name="in_proj",
    )(x2, norm_g, w_in_b, q_norm_g, k_norm_g, cos2, sin2)


def _attn_kernel(q_ref, kp_ref, km_ref, kn_ref, vp_ref, vm_ref, vn_ref, o_ref, lse_ref,
                 kx_ref, vx_ref, *, tq, sub, radius, length):
    i = pl.program_id(2)
    kx_ref[0:radius, :] = kp_ref[...]
    kx_ref[radius:radius + tq, :] = km_ref[...]
    kx_ref[radius + tq:, :] = kn_ref[...]
    vx_ref[0:radius, :] = vp_ref[...]
    vx_ref[radius:radius + tq, :] = vm_ref[...]
    vx_ref[radius + tq:, :] = vn_ref[...]
    nk = sub + 2 * radius
    scale = HEAD_DIM ** -0.5
    rel = lax.broadcasted_iota(jnp.int32, (sub, nk), 1) - radius - lax.broadcasted_iota(jnp.int32, (sub, nk), 0)
    band = jnp.abs(rel) <= radius
    for c in range(tq // sub):
        kpos = i * tq + c * sub - radius + lax.broadcasted_iota(jnp.int32, (sub, nk), 1)
        valid = band & (kpos >= 0) & (kpos < length)
        for h in range(HEADS_PER_GROUP):
            cols = slice(h * HEAD_DIM, (h + 1) * HEAD_DIM)
            q = q_ref[c * sub:(c + 1) * sub, cols]
            k = kx_ref[c * sub:c * sub + nk, cols]
            v = vx_ref[c * sub:c * sub + nk, cols]
            s = lax.dot_general(q, k, (((1,), (1,)), ((), ())), preferred_element_type=F32) * scale
            s = jnp.where(valid, s, NEG_INF)
            m = jnp.max(s, axis=-1, keepdims=True)
            p = jnp.exp(s - m)
            denom = jnp.sum(p, axis=-1, keepdims=True)
            o = jnp.dot(p.astype(BF16), v, preferred_element_type=F32) / denom
            o_ref[c * sub:(c + 1) * sub, cols] = o.astype(o_ref.dtype)
            lse = m + jnp.log(denom)
            lse_ref[c * sub:(c + 1) * sub, cols] = jnp.broadcast_to(lse, (sub, HEAD_DIM))


def _attention_group(qkv, group, window, dilation):
    _, batch, _, length, _ = qkv.shape
    radius = window // (2 * dilation)
    tq = _tile(length, 512)
    sub = min(tq, 2 * radius)
    assert tq % sub == 0 and tq % radius == 0 and length % radius == 0
    halo_per_tile = tq // radius
    n_halo = length // radius

    def main(which):
        return pl.BlockSpec((None, None, None, tq, GROUP_WIDTH), lambda b, r, i: (which, b, r, i, 0))

    def prev(which):
        return pl.BlockSpec((None, None, None, radius, GROUP_WIDTH),
                            lambda b, r, i: (which, b, r, jnp.maximum(i * halo_per_tile - 1, 0), 0))

    def nxt(which):
        return pl.BlockSpec((None, None, None, radius, GROUP_WIDTH),
                            lambda b, r, i: (which, b, r, jnp.minimum((i + 1) * halo_per_tile, n_halo - 1), 0))

    out_spec = pl.BlockSpec((None, None, tq, GROUP_WIDTH), lambda b, r, i: (b, r, i, 0))
    out_shape = (batch, dilation, length, GROUP_WIDTH)
    return pl.pallas_call(
        functools.partial(_attn_kernel, tq=tq, sub=sub, radius=radius, length=length),
        out_shape=(jax.ShapeDtypeStruct(out_shape, BF16), jax.ShapeDtypeStruct(out_shape, F32)),
        grid=(batch, dilation, length // tq),
        in_specs=[main(0), prev(1), main(1), nxt(1), prev(2), main(2), nxt(2)],
        out_specs=(out_spec, out_spec),
        scratch_shapes=[pltpu.VMEM((tq + 2 * radius, GROUP_WIDTH), BF16),
                        pltpu.VMEM((tq + 2 * radius, GROUP_WIDTH), BF16)],
        compiler_params=_params(("parallel", "parallel", "arbitrary")),
        name=f"attn_g{group}",
    )(qkv, qkv, qkv, qkv, qkv, qkv, qkv)


def _merge_kernel(x_ref, g_ref, up_ref, um_ref, un_ref,
                  o0_ref, o1_ref, o2_ref, l0_ref, l1_ref, l2_ref,
                  bg_ref, wp_ref, ps_ref, wpo_ref, wao_ref, wo_ref, fg_ref,
                  x1_ref, h2_ref, ux_ref, oi_ref, li_ref, *, tm, seq):
    i = pl.program_id(0)
    d = x_ref.shape[1]
    pool_w = um_ref.shape[1]
    n_pool = len(POOL_WINDOWS)
    gdim = pool_w // n_pool

    pos0 = (i * tm) % seq
    pos_p = pos0 - HALO + lax.broadcasted_iota(jnp.int32, (HALO, 1), 0)
    pos_n = pos0 + tm + lax.broadcasted_iota(jnp.int32, (HALO, 1), 0)
    ux_ref[0:HALO, :] = jnp.where(pos_p >= 0, up_ref[...].astype(F32), 0.0)
    ux_ref[HALO:HALO + tm, :] = um_ref[...].astype(F32)
    ux_ref[HALO + tm:, :] = jnp.where(pos_n < seq, un_ref[...].astype(F32), 0.0)

    pos = pos0 + lax.broadcasted_iota(jnp.int32, (tm, 1), 0)
    mixed = []
    for g, window in enumerate(POOL_WINDOWS):
        rad = window // 2
        cols = slice(g * gdim, (g + 1) * gdim)
        u = ux_ref[HALO:HALO + tm, cols]
        tot = u
        for t in range(1, rad + 1):
            tot = tot + ux_ref[HALO - t:HALO - t + tm, cols] + ux_ref[HALO + t:HALO + t + tm, cols]
        count = (jnp.minimum(pos + rad + 1, seq) - jnp.maximum(pos - rad, 0)).astype(F32)
        centred = tot / count - u
        mg = jnp.dot(centred.astype(BF16), wp_ref[g], preferred_element_type=F32)
        mixed.append((mg * ps_ref[:, cols]).astype(BF16))
    pool_d = jnp.dot(jnp.concatenate(mixed, axis=1), wpo_ref[...], preferred_element_type=F32)

    outs, lses = [], []
    for g, (o_ref, l_ref) in enumerate(((o0_ref, l0_ref), (o1_ref, l1_ref), (o2_ref, l2_ref))):
        dilation = o_ref.shape[0]
        if dilation == 1:
            outs.append(o_ref[0].astype(F32))
            lses.append(l_ref[0])
        else:
            for h in range(HEADS_PER_GROUP):
                cols = slice(h * HEAD_DIM, (h + 1) * HEAD_DIM)
                for r in range(dilation):
                    rows = pl.ds(r, tm // dilation, stride=dilation)
                    oi_ref[g, h, rows, :] = o_ref[r, :, cols].astype(F32)
                    li_ref[g, h, rows, :] = l_ref[r, :, cols]
            outs.append(jnp.concatenate([oi_ref[g, h] for h in range(HEADS_PER_GROUP)], axis=1))
            lses.append(jnp.concatenate([li_ref[g, h] for h in range(HEADS_PER_GROUP)], axis=1))
    l0, l1, l2 = lses
    lmax = jnp.maximum(jnp.maximum(l0, l1), l2)
    e0, e1, e2 = jnp.exp(l0 - lmax), jnp.exp(l1 - lmax), jnp.exp(l2 - lmax)
    attn = (e0 * outs[0] + e1 * outs[1] + e2 * outs[2]) / (e0 + e1 + e2)
    attn_d = jnp.dot(attn.astype(BF16), wao_ref[...], preferred_element_type=F32)

    gates = jax.nn.sigmoid(g_ref[...].astype(F32) + bg_ref[...])
    merged = gates[:, :d] * pool_d + gates[:, d:] * attn_d
    x1 = x_ref[...] + jnp.dot(merged.astype(BF16), wo_ref[...], preferred_element_type=F32)
    x1_ref[...] = x1
    inv = lax.rsqrt(jnp.mean(x1 * x1, axis=-1, keepdims=True) + EPS)
    h2_ref[...] = (x1 * inv * fg_ref[...]).astype(BF16)


def _merge(x2, z, attn_outs, b_gate, w_pool_b, pool_scale, w_pool_out_b, w_attn_out_b, w_o_b,
           ffn_norm_g, seq):
    t, d = x2.shape
    pool_w = w_pool_out_b.shape[0]
    tm = _tile(seq, 256)
    halo_per_tile = tm // HALO
    n_halo = t // HALO
    u_col = (2 * d) // pool_w

    row = lambda i: (i, 0)
    pos_tiles = seq // tm
    (o0, l0), (o1, l1), (o2, l2) = attn_outs
    grp = []
    for _, dilation in ATTN_GROUPS:
        assert tm % (dilation * BF16_SUBLANES) == 0
        grp.append(pl.BlockSpec((None, dilation, tm // dilation, GROUP_WIDTH),
                                lambda i: (i // pos_tiles, 0, i % pos_tiles, 0)))
    return pl.pallas_call(
        functools.partial(_merge_kernel, tm=tm, seq=seq),
        out_shape=(jax.ShapeDtypeStruct((t, d), F32), jax.ShapeDtypeStruct((t, d), BF16)),
        grid=(t // tm,),
        in_specs=[
            pl.BlockSpec((tm, d), row),
            pl.BlockSpec((tm, 2 * d), row),
            pl.BlockSpec((HALO, pool_w), lambda i: (jnp.maximum(i * halo_per_tile - 1, 0), u_col)),
            pl.BlockSpec((tm, pool_w), lambda i: (i, u_col)),
            pl.BlockSpec((HALO, pool_w), lambda i: (jnp.minimum((i + 1) * halo_per_tile, n_halo - 1), u_col)),
            grp[0], grp[1], grp[2], grp[0], grp[1], grp[2],
            _const_spec(b_gate.shape),
            _const_spec(w_pool_b.shape),
            _const_spec(pool_scale.shape),
            _const_spec(w_pool_out_b.shape),
            _const_spec(w_attn_out_b.shape),
            _const_spec(w_o_b.shape),
            _const_spec(ffn_norm_g.shape),
        ],
        out_specs=(pl.BlockSpec((tm, d), row), pl.BlockSpec((tm, d), row)),
        scratch_shapes=[pltpu.VMEM((tm + 2 * HALO, pool_w), F32),
                        pltpu.VMEM((len(ATTN_GROUPS), HEADS_PER_GROUP, tm, HEAD_DIM), F32),
                        pltpu.VMEM((len(ATTN_GROUPS), HEADS_PER_GROUP, tm, HEAD_DIM), F32)],
        compiler_params=_params(("parallel",)),
        name="merge",
    )(x2, z, z, z, z, o0, o1, o2, l0, l1, l2,
      b_gate, w_pool_b, pool_scale, w_pool_out_b, w_attn_out_b, w_o_b, ffn_norm_g)


def _ffn_kernel(hp_ref, hm_ref, hn_ref, x1_ref, wg_ref, wv_ref, cw_ref, cb_ref, wd_ref,
                y_ref, hx_ref, *, tm, seq):
    i = pl.program_id(0)
    f = pl.program_id(1)

    @pl.when(f == 0)
    def _():
        pos0 = (i * tm) % seq
        pos_p = pos0 - HALO + lax.broadcasted_iota(jnp.int32, (HALO, 1), 0)
        pos_n = pos0 + tm + lax.broadcasted_iota(jnp.int32, (HALO, 1), 0)
        hx_ref[0:HALO, :] = jnp.where(pos_p >= 0, hp_ref[...], jnp.zeros_like(hp_ref[...]))
        hx_ref[HALO:HALO + tm, :] = hm_ref[...]
        hx_ref[HALO + tm:, :] = jnp.where(pos_n < seq, hn_ref[...], jnp.zeros_like(hn_ref[...]))
        y_ref[...] = x1_ref[...]

    gate = jnp.dot(hx_ref[...], wg_ref[...], preferred_element_type=F32)
    val = jnp.dot(hm_ref[...], wv_ref[...], preferred_element_type=F32)
    half = CONV_WIDTH // 2
    conv = cb_ref[...]
    for tap in range(CONV_WIDTH):
        lo = HALO - half + tap
        conv = conv + gate[lo:lo + tm, :] * cw_ref[tap:tap + 1, :]
    act = jax.nn.gelu(conv) * val
    y_ref[...] += jnp.dot(act.astype(BF16), wd_ref[...], preferred_element_type=F32)


def _ffn(h2, x1, w_up_b, conv_w, conv_b, w_down_b, seq):
    t, d = x1.shape
    d_ff = w_down_b.shape[0]
    tm = _tile(seq, 512)
    tf = w_up_b.shape[2]
    nf = d_ff // tf
    halo_per_tile = tm // HALO
    n_halo = t // HALO
    return pl.pallas_call(
        functools.partial(_ffn_kernel, tm=tm, seq=seq),
        out_shape=jax.ShapeDtypeStruct((t, d), F32),
        grid=(t // tm, nf),
        in_specs=[
            pl.BlockSpec((HALO, d), lambda i, f: (jnp.maximum(i * halo_per_tile - 1, 0), 0)),
            pl.BlockSpec((tm, d), lambda i, f: (i, 0)),
            pl.BlockSpec((HALO, d), lambda i, f: (jnp.minimum((i + 1) * halo_per_tile, n_halo - 1), 0)),
            pl.BlockSpec((tm, d), lambda i, f: (i, 0)),
            pl.BlockSpec((None, d, tf), lambda i, f: (f, 0, 0)),
            pl.BlockSpec((None, d, tf), lambda i, f: (nf + f, 0, 0)),
            pl.BlockSpec((CONV_WIDTH, tf), lambda i, f: (0, f)),
            pl.BlockSpec((1, tf), lambda i, f: (0, f)),
            pl.BlockSpec((tf, d), lambda i, f: (f, 0)),
        ],
        out_specs=pl.BlockSpec((tm, d), lambda i, f: (i, 0)),
        scratch_shapes=[pltpu.VMEM((tm + 2 * HALO, d), BF16)],
        compiler_params=_params(("parallel", "arbitrary")),
        name="ffn",
    )(h2, h2, h2, x1, w_up_b, w_up_b, conv_w, conv_b, w_down_b)


def _rope_tables(seq):
    half = HEAD_DIM // 2
    inv_freq = ROPE_THETA ** (-jnp.arange(half, dtype=F32) / half)
    ang = jnp.arange(seq, dtype=F32)[:, None] * inv_freq[None, :]
    cos, sin = jnp.cos(ang), jnp.sin(ang)
    return jnp.concatenate([cos, cos], axis=-1), jnp.concatenate([-sin, sin], axis=-1)


def _layer(x, p):
    batch, seq, d = x.shape
    x2 = x.reshape(batch * seq, d)
    cos2, sin2 = _rope_tables(seq)
    q_lo = (2 * d + p["pool_w"]) // COL_TILE
    z, *qkv = _in_proj(x2, p["mix_norm_g"], p["w_in"], p["q_norm_g"], p["k_norm_g"], cos2, sin2,
                       batch, seq, q_lo)
    attn_outs = [_attention_group(qkv[g], g, window, dilation)
                 for g, (window, dilation) in enumerate(ATTN_GROUPS)]
    x1, h2 = _merge(x2, z, attn_outs, p["b_gate"], p["w_pool"], p["pool_scale"], p["w_pool_out"],
                    p["w_attn_out"], p["w_o"], p["ffn_norm_g"], seq)
    y = _ffn(h2, x1, p["w_up"], p["conv_w"], p["conv_b"], p["w_down"], seq)
    return y.reshape(batch, seq, d)


def kernel(x_prompt, x_sample, mix_norm_g, w_in, b_gate, q_norm_g, k_norm_g, w_pool, pool_scale,
           w_pool_out, w_attn_out, w_o, ffn_norm_g, w_up, conv_w, conv_b, w_down):
    depth = w_in.shape[0]
    pool_w = w_pool_out.shape[1]
    d_ff = w_down.shape[1]
    ffn_tile = 512 if d_ff % 512 == 0 else 256

    def col_tiles(w, tile):
        rows, cols = w.shape
        return w.astype(BF16).reshape(rows, cols // tile, tile).transpose(1, 0, 2)

    layers = []
    for l in range(depth):
        layers.append(dict(
            pool_w=pool_w,
            mix_norm_g=mix_norm_g[l][None, :], w_in=col_tiles(w_in[l], COL_TILE), b_gate=b_gate[l][None, :],
            q_norm_g=q_norm_g[l][None, :], k_norm_g=k_norm_g[l][None, :],
            w_pool=w_pool[l].astype(BF16), pool_scale=pool_scale[l][None, :],
            w_pool_out=w_pool_out[l].astype(BF16), w_attn_out=w_attn_out[l].astype(BF16),
            w_o=w_o[l].astype(BF16), ffn_norm_g=ffn_norm_g[l][None, :],
            w_up=col_tiles(w_up[l], ffn_tile), conv_w=conv_w[l], conv_b=conv_b[l][None, :],
            w_down=w_down[l].astype(BF16)))

    def trunk(x):
        for p in layers:
            x = _layer(x, p)
        return x

    return (trunk(x_prompt), trunk(x_sample))
```

```python
import functools

import jax
import jax.numpy as jnp
from jax import lax
from jax.experimental import pallas as pl
from jax.experimental.pallas import tpu as pltpu

F32 = jnp.float32
BF16 = jnp.bfloat16

POOL_WINDOWS = (2, 4, 8, 16)
HEAD_DIM = 128
ATTN_GROUPS = ((128, 1), (512, 4), (2048, 16))
HEADS_PER_GROUP = 4
GROUP_WIDTH = HEADS_PER_GROUP * HEAD_DIM
ATTN_WIDTH = len(ATTN_GROUPS) * GROUP_WIDTH
ROPE_THETA = 10000.0
CONV_WIDTH = 3
EPS = 1e-6
NEG_INF = -1e30

LANES = 128
BF16_SUBLANES = 16
V7X_VMEM_BYTES = 64 * 1024 * 1024
VMEM_LIMIT_BYTES = V7X_VMEM_BYTES - 8 * 1024 * 1024

COL_TILE = GROUP_WIDTH
HALO = BF16_SUBLANES


def _tile(n, target):
    t = min(n, target)
    while n % t or t % BF16_SUBLANES:
        t -= 1
    return t


def _const_spec(shape):
    nd = len(shape)
    return pl.BlockSpec(shape, lambda *_: (0,) * nd, pipeline_mode=pl.Buffered(1))


def _params(semantics):
    return pltpu.CompilerParams(dimension_semantics=semantics, vmem_limit_bytes=VMEM_LIMIT_BYTES)


def _in_proj_kernel(x_ref, ng_ref, w_ref, qg_ref, kg_ref, cos_ref, sin_ref,
                    zr_ref, qkv0_ref, qkv1_ref, qkv2_ref, h_ref, acc0_ref, acc1_ref, de_ref,
                    *, q_lo, n_tiles):
    s = pl.program_id(1)
    tm = x_ref.shape[0]
    n_grp = len(ATTN_GROUPS)
    accs = (acc0_ref, acc1_ref)

    def project(tile):
        res = jnp.dot(h_ref[...], w_ref[...], preferred_element_type=F32)
        for h in range(HEADS_PER_GROUP):
            accs[tile % 2][h] = res[:, h * HEAD_DIM:(h + 1) * HEAD_DIM]

    @pl.when(s == 0)
    def _():
        x = x_ref[...]
        inv = lax.rsqrt(jnp.mean(x * x, axis=-1, keepdims=True) + EPS)
        h_ref[...] = (x * inv * ng_ref[...]).astype(BF16)
        project(0)

    for parity in range(2):
        @pl.when((s >= 1) & (s <= q_lo) & ((s - 1) % 2 == parity))
        def _(parity=parity):
            project(parity + 1)
            for h in range(HEADS_PER_GROUP):
                zr_ref[:, h * HEAD_DIM:(h + 1) * HEAD_DIM] = accs[parity][h].astype(BF16)

    def normed_rope(a, gain):
        inv = lax.rsqrt(jnp.mean(a * a, axis=-1, keepdims=True) + EPS)
        a = a * inv * gain
        rot = pltpu.roll(a, HEAD_DIM // 2, axis=1)
        return a * cos_ref[...] + rot * sin_ref[...]

    for which in range(3):
        for g, (out_ref, (_, dilation)) in enumerate(zip((qkv0_ref, qkv1_ref, qkv2_ref), ATTN_GROUPS)):
            tile = q_lo + which * n_grp + g

            @pl.when(s == tile + 1)
            def _(which=which, out_ref=out_ref, dilation=dilation, tile=tile):
                if tile + 1 < n_tiles:
                    project(tile + 1)
                acc_ref = accs[tile % 2]
                for h in range(HEADS_PER_GROUP):
                    cols = slice(h * HEAD_DIM, (h + 1) * HEAD_DIM)
                    src = acc_ref
                    if which < 2:
                        a = normed_rope(acc_ref[h], qg_ref[...] if which == 0 else kg_ref[...])
                        if dilation == 1:
                            out_ref[which, 0, :, cols] = a.astype(BF16)
                            continue
                        de_ref[h] = a
                        src = de_ref
                    for r in range(dilation):
                        rows = src[h, pl.ds(r, tm // dilation, stride=dilation), :]
                        out_ref[which, r, :, cols] = rows.astype(BF16)


def _in_proj(x2, norm_g, w_in_b, q_norm_g, k_norm_g, cos2, sin2, batch, seq, q_lo):
    t, d = x2.shape
    n_tiles = w_in_b.shape[1] // COL_TILE
    tm = _tile(seq, 1024)
    pos_tiles = seq // tm
    n_gate = (2 * d) // COL_TILE
    grid = (t // tm, n_tiles + 1)
    qkv_shapes, qkv_specs = [], []
    for _, dilation in ATTN_GROUPS:
        assert tm % (dilation * BF16_SUBLANES) == 0
        qkv_shapes.append(jax.ShapeDtypeStruct((3, batch, dilation, seq // dilation, GROUP_WIDTH), BF16))
        qkv_specs.append(pl.BlockSpec((3, None, dilation, tm // dilation, GROUP_WIDTH),
                                      lambda i, j: (0, i // pos_tiles, 0, i % pos_tiles, 0)))
    return pl.pallas_call(
        functools.partial(_in_proj_kernel, q_lo=q_lo, n_tiles=n_tiles),
        out_shape=[jax.ShapeDtypeStruct((q_lo, t, COL_TILE), BF16)] + qkv_shapes,
        grid=grid,
        in_specs=[
            pl.BlockSpec((tm, d), lambda i, j: (i, 0)),
            pl.BlockSpec((1, d), lambda i, j: (0, 0)),
            pl.BlockSpec((d, COL_TILE),
                         lambda i, j: (0, (jnp.minimum(j, n_tiles - 1) + n_tiles - n_gate) % n_tiles)),
            pl.BlockSpec((1, HEAD_DIM), lambda i, j: (0, 0)),
            pl.BlockSpec((1, HEAD_DIM), lambda i, j: (0, 0)),
            pl.BlockSpec((tm, HEAD_DIM), lambda i, j: (i % pos_tiles, 0)),
            pl.BlockSpec((tm, HEAD_DIM), lambda i, j: (i % pos_tiles, 0)),
        ],
        out_specs=[pl.BlockSpec((None, tm, COL_TILE),
                                lambda i, j: (jnp.clip(j - 1, 0, q_lo - 1), i, 0))] + qkv_specs,
        scratch_shapes=[pltpu.VMEM((tm, d), BF16)] + [pltpu.VMEM((HEADS_PER_GROUP, tm, HEAD_DIM), F32)] * 3,
        compiler_params=_params(("parallel", "arbitrary")),
        name="in_proj",
    )(x2, norm_g, w_in_b, q_norm_g, k_norm_g, cos2, sin2)


def _attn_kernel(q_ref, kp_ref, km_ref, kn_ref, vp_ref, vm_ref, vn_ref, o_ref, lse_ref,
                 kx_ref, vx_ref, *, tq, sub, radius, length):
    i = pl.program_id(2)
    kx_ref[0:radius, :] = kp_ref[...]
    kx_ref[radius:radius + tq, :] = km_ref[...]
    kx_ref[radius + tq:, :] = kn_ref[...]
    vx_ref[0:radius, :] = vp_ref[...]
    vx_ref[radius:radius + tq, :] = vm_ref[...]
    vx_ref[radius + tq:, :] = vn_ref[...]
    nk = sub + 2 * radius
    scale = HEAD_DIM ** -0.5
    rel = lax.broadcasted_iota(jnp.int32, (sub, nk), 1) - radius - lax.broadcasted_iota(jnp.int32, (sub, nk), 0)
    band = jnp.abs(rel) <= radius
    for c in range(tq // sub):
        kpos = i * tq + c * sub - radius + lax.broadcasted_iota(jnp.int32, (sub, nk), 1)
        valid = band & (kpos >= 0) & (kpos < length)
        for h in range(HEADS_PER_GROUP):
            cols = slice(h * HEAD_DIM, (h + 1) * HEAD_DIM)
            q = q_ref[c * sub:(c + 1) * sub, cols]
            k = kx_ref[c * sub:c * sub + nk, cols]
            v = vx_ref[c * sub:c * sub + nk, cols]
            s = lax.dot_general(q, k, (((1,), (1,)), ((), ())), preferred_element_type=F32) * scale
            s = jnp.where(valid, s, NEG_INF)
            m = jnp.max(s, axis=-1, keepdims=True)
            p = jnp.exp(s - m)
            denom = jnp.sum(p, axis=-1, keepdims=True)
            o = jnp.dot(p.astype(BF16), v, preferred_element_type=F32) / denom
            o_ref[c * sub:(c + 1) * sub, cols] = o.astype(o_ref.dtype)
            lse = m + jnp.log(denom)
            lse_ref[c * sub:(c + 1) * sub, cols] = jnp.broadcast_to(lse, (sub, HEAD_DIM))


def _attention_group(qkv, group, window, dilation):
    _, batch, _, length, _ = qkv.shape
    radius = window // (2 * dilation)
    tq = _tile(length, 512)
    sub = min(tq, 2 * radius)
    assert tq % sub == 0 and tq % radius == 0 and length % radius == 0
    halo_per_tile = tq // radius
    n_halo = length // radius

    def main(which):
        return pl.BlockSpec((None, None, None, tq, GROUP_WIDTH), lambda b, r, i: (which, b, r, i, 0))

    def prev(which):
        return pl.BlockSpec((None, None, None, radius, GROUP_WIDTH),
                            lambda b, r, i: (which, b, r, jnp.maximum(i * halo_per_tile - 1, 0), 0))

    def nxt(which):
        return pl.BlockSpec((None, None, None, radius, GROUP_WIDTH),
                            lambda b, r, i: (which, b, r, jnp.minimum((i + 1) * halo_per_tile, n_halo - 1), 0))

    out_spec = pl.BlockSpec((None, None, tq, GROUP_WIDTH), lambda b, r, i: (b, r, i, 0))
    out_shape = (batch, dilation, length, GROUP_WIDTH)
    return pl.pallas_call(
        functools.partial(_attn_kernel, tq=tq, sub=sub, radius=radius, length=length),
        out_shape=(jax.ShapeDtypeStruct(out_shape, BF16), jax.ShapeDtypeStruct(out_shape, F32)),
        grid=(batch, dilation, length // tq),
        in_specs=[main(0), prev(1), main(1), nxt(1), prev(2), main(2), nxt(2)],
        out_specs=(out_spec, out_spec),
        scratch_shapes=[pltpu.VMEM((tq + 2 * radius, GROUP_WIDTH), BF16),
                        pltpu.VMEM((tq + 2 * radius, GROUP_WIDTH), BF16)],
        compiler_params=_params(("parallel", "parallel", "arbitrary")),
        name=f"attn_g{group}",
    )(qkv, qkv, qkv, qkv, qkv, qkv, qkv)


def _merge_kernel(x_ref, g_ref, up_ref, um_ref, un_ref,
                  o0_ref, o1_ref, o2_ref, l0_ref, l1_ref, l2_ref,
                  bg_ref, wp_ref, ps_ref, wpo_ref, wao_ref, wo_ref, fg_ref,
                  x1_ref, h2_ref, ux_ref, oi_ref, li_ref, *, tm, seq):
    i = pl.program_id(0)
    d = x_ref.shape[1]
    pool_w = um_ref.shape[0] * um_ref.shape[2]
    n_pool = len(POOL_WINDOWS)
    gdim = pool_w // n_pool

    def wide(ref):
        return jnp.concatenate([ref[k] for k in range(ref.shape[0])], axis=1)

    pos0 = (i * tm) % seq
    pos_p = pos0 - HALO + lax.broadcasted_iota(jnp.int32, (HALO, 1), 0)
    pos_n = pos0 + tm + lax.broadcasted_iota(jnp.int32, (HALO, 1), 0)
    ux_ref[0:HALO, :] = jnp.where(pos_p >= 0, wide(up_ref).astype(F32), 0.0)
    ux_ref[HALO:HALO + tm, :] = wide(um_ref).astype(F32)
    ux_ref[HALO + tm:, :] = jnp.where(pos_n < seq, wide(un_ref).astype(F32), 0.0)

    pos = pos0 + lax.broadcasted_iota(jnp.int32, (tm, 1), 0)
    mixed = []
    for g, window in enumerate(POOL_WINDOWS):
        rad = window // 2
        cols = slice(g * gdim, (g + 1) * gdim)
        u = ux_ref[HALO:HALO + tm, cols]
        tot = u
        for t in range(1, rad + 1):
            tot = tot + ux_ref[HALO - t:HALO - t + tm, cols] + ux_ref[HALO + t:HALO + t + tm, cols]
        count = (jnp.minimum(pos + rad + 1, seq) - jnp.maximum(pos - rad, 0)).astype(F32)
        centred = tot / count - u
        mg = jnp.dot(centred.astype(BF16), wp_ref[g], preferred_element_type=F32)
        mixed.append((mg * ps_ref[:, cols]).astype(BF16))
    pool_d = jnp.dot(jnp.concatenate(mixed, axis=1), wpo_ref[...], preferred_element_type=F32)

    outs, lses = [], []
    for g, (o_ref, l_ref) in enumerate(((o0_ref, l0_ref), (o1_ref, l1_ref), (o2_ref, l2_ref))):
        dilation = o_ref.shape[0]
        if dilation == 1:
            outs.append(o_ref[0].astype(F32))
            lses.append(l_ref[0])
        else:
            for h in range(HEADS_PER_GROUP):
                cols = slice(h * HEAD_DIM, (h + 1) * HEAD_DIM)
                for r in range(dilation):
                    rows = pl.ds(r, tm // dilation, stride=dilation)
                    oi_ref[g, h, rows, :] = o_ref[r, :, cols].astype(F32)
                    li_ref[g, h, rows, :] = l_ref[r, :, cols]
            outs.append(jnp.concatenate([oi_ref[g, h] for h in range(HEADS_PER_GROUP)], axis=1))
            lses.append(jnp.concatenate([li_ref[g, h] for h in range(HEADS_PER_GROUP)], axis=1))
    l0, l1, l2 = lses
    lmax = jnp.maximum(jnp.maximum(l0, l1), l2)
    e0, e1, e2 = jnp.exp(l0 - lmax), jnp.exp(l1 - lmax), jnp.exp(l2 - lmax)
    attn = (e0 * outs[0] + e1 * outs[1] + e2 * outs[2]) / (e0 + e1 + e2)
    attn_d = jnp.dot(attn.astype(BF16), wao_ref[...], preferred_element_type=F32)

    gates = jax.nn.sigmoid(wide(g_ref).astype(F32) + bg_ref[...])
    merged = gates[:, :d] * pool_d + gates[:, d:] * attn_d
    x1 = x_ref[...] + jnp.dot(merged.astype(BF16), wo_ref[...], preferred_element_type=F32)
    x1_ref[...] = x1
    inv = lax.rsqrt(jnp.mean(x1 * x1, axis=-1, keepdims=True) + EPS)
    h2_ref[...] = (x1 * inv * fg_ref[...]).astype(BF16)


def _merge(x2, z, attn_outs, b_gate, w_pool_b, pool_scale, w_pool_out_b, w_attn_out_b, w_o_b,
           ffn_norm_g, seq):
    t, d = x2.shape
    pool_w = w_pool_out_b.shape[0]
    tm = _tile(seq, 256)
    halo_per_tile = tm // HALO
    n_halo = t // HALO
    n_gate = (2 * d) // COL_TILE
    n_u = pool_w // COL_TILE
    assert n_gate % n_u == 0
    u_blk = n_gate // n_u

    row = lambda i: (i, 0)
    pos_tiles = seq // tm
    (o0, l0), (o1, l1), (o2, l2) = attn_outs
    grp = []
    for _, dilation in ATTN_GROUPS:
        assert tm % (dilation * BF16_SUBLANES) == 0
        grp.append(pl.BlockSpec((None, dilation, tm // dilation, GROUP_WIDTH),
                                lambda i: (i // pos_tiles, 0, i % pos_tiles, 0)))
    return pl.pallas_call(
        functools.partial(_merge_kernel, tm=tm, seq=seq),
        out_shape=(jax.ShapeDtypeStruct((t, d), F32), jax.ShapeDtypeStruct((t, d), BF16)),
        grid=(t // tm,),
        in_specs=[
            pl.BlockSpec((tm, d), row),
            pl.BlockSpec((n_gate, tm, COL_TILE), lambda i: (0, i, 0)),
            pl.BlockSpec((n_u, HALO, COL_TILE), lambda i: (u_blk, jnp.maximum(i * halo_per_tile - 1, 0), 0)),
            pl.BlockSpec((n_u, tm, COL_TILE), lambda i: (u_blk, i, 0)),
            pl.BlockSpec((n_u, HALO, COL_TILE),
                         lambda i: (u_blk, jnp.minimum((i + 1) * halo_per_tile, n_halo - 1), 0)),
            grp[0], grp[1], grp[2], grp[0], grp[1], grp[2],
            _const_spec(b_gate.shape),
            _const_spec(w_pool_b.shape),
            _const_spec(pool_scale.shape),
            _const_spec(w_pool_out_b.shape),
            _const_spec(w_attn_out_b.shape),
            _const_spec(w_o_b.shape),
            _const_spec(ffn_norm_g.shape),
        ],
        out_specs=(pl.BlockSpec((tm, d), row), pl.BlockSpec((tm, d), row)),
        scratch_shapes=[pltpu.VMEM((tm + 2 * HALO, pool_w), F32),
                        pltpu.VMEM((len(ATTN_GROUPS), HEADS_PER_GROUP, tm, HEAD_DIM), F32),
                        pltpu.VMEM((len(ATTN_GROUPS), HEADS_PER_GROUP, tm, HEAD_DIM), F32)],
        compiler_params=_params(("parallel",)),
        name="merge",
    )(x2, z, z, z, z, o0, o1, o2, l0, l1, l2,
      b_gate, w_pool_b, pool_scale, w_pool_out_b, w_attn_out_b, w_o_b, ffn_norm_g)


def _ffn_kernel(hp_ref, hm_ref, hn_ref, x1_ref, wg_ref, wv_ref, cw_ref, cb_ref, wd_ref,
                y_ref, hx_ref, *, tm, seq):
    i = pl.program_id(0)
    f = pl.program_id(1)

    @pl.when(f == 0)
    def _():
        pos0 = (i * tm) % seq
        pos_p = pos0 - HALO + lax.broadcasted_iota(jnp.int32, (HALO, 1), 0)
        pos_n = pos0 + tm + lax.broadcasted_iota(jnp.int32, (HALO, 1), 0)
        hx_ref[0:HALO, :] = jnp.where(pos_p >= 0, hp_ref[...], jnp.zeros_like(hp_ref[...]))
        hx_ref[HALO:HALO + tm, :] = hm_ref[...]
        hx_ref[HALO + tm:, :] = jnp.where(pos_n < seq, hn_ref[...], jnp.zeros_like(hn_ref[...]))
        y_ref[...] = x1_ref[...]

    gate = jnp.dot(hx_ref[...], wg_ref[...], preferred_element_type=F32)
    val = jnp.dot(hm_ref[...], wv_ref[...], preferred_element_type=F32)
    half = CONV_WIDTH // 2
    conv = cb_ref[...]
    for tap in range(CONV_WIDTH):
        lo = HALO - half + tap
        conv = conv + gate[lo:lo + tm, :] * cw_ref[tap:tap + 1, :]
    act = jax.nn.gelu(conv) * val
    y_ref[...] += jnp.dot(act.astype(BF16), wd_ref[...], preferred_element_type=F32)


def _ffn(h2, x1, w_up_b, conv_w, conv_b, w_down_b, seq):
    t, d = x1.shape
    d_ff = w_down_b.shape[0]
    tm = _tile(seq, 512)
    tf = 512 if d_ff % 512 == 0 else 256
    nf = d_ff // tf
    halo_per_tile = tm // HALO
    n_halo = t // HALO
    return pl.pallas_call(
        functools.partial(_ffn_kernel, tm=tm, seq=seq),
        out_shape=jax.ShapeDtypeStruct((t, d), F32),
        grid=(t // tm, nf),
        in_specs=[
            pl.BlockSpec((HALO, d), lambda i, f: (jnp.maximum(i * halo_per_tile - 1, 0), 0)),
            pl.BlockSpec((tm, d), lambda i, f: (i, 0)),
            pl.BlockSpec((HALO, d), lambda i, f: (jnp.minimum((i + 1) * halo_per_tile, n_halo - 1), 0)),
            pl.BlockSpec((tm, d), lambda i, f: (i, 0)),
            pl.BlockSpec((d, tf), lambda i, f: (0, f)),
            pl.BlockSpec((d, tf), lambda i, f: (0, nf + f)),
            pl.BlockSpec((CONV_WIDTH, tf), lambda i, f: (0, f)),
            pl.BlockSpec((1, tf), lambda i, f: (0, f)),
            pl.BlockSpec((tf, d), lambda i, f: (f, 0)),
        ],
        out_specs=pl.BlockSpec((tm, d), lambda i, f: (i, 0)),
        scratch_shapes=[pltpu.VMEM((tm + 2 * HALO, d), BF16)],
        compiler_params=_params(("parallel", "arbitrary")),
        name="ffn",
    )(h2, h2, h2, x1, w_up_b, w_up_b, conv_w, conv_b, w_down_b)


def _rope_tables(seq):
    half = HEAD_DIM // 2
    inv_freq = ROPE_THETA ** (-jnp.arange(half, dtype=F32) / half)
    ang = jnp.arange(seq, dtype=F32)[:, None] * inv_freq[None, :]
    cos, sin = jnp.cos(ang), jnp.sin(ang)
    return jnp.concatenate([cos, cos], axis=-1), jnp.concatenate([-sin, sin], axis=-1)


def _layer(x, p):
    batch, seq, d = x.shape
    x2 = x.reshape(batch * seq, d)
    cos2, sin2 = _rope_tables(seq)
    q_lo = (2 * d + p["pool_w"]) // COL_TILE
    z, *qkv = _in_proj(x2, p["mix_norm_g"], p["w_in"], p["q_norm_g"], p["k_norm_g"], cos2, sin2,
                       batch, seq, q_lo)
    attn_outs = [_attention_group(qkv[g], g, window, dilation)
                 for g, (window, dilation) in enumerate(ATTN_GROUPS)]
    x1, h2 = _merge(x2, z, attn_outs, p["b_gate"], p["w_pool"], p["pool_scale"], p["w_pool_out"],
                    p["w_attn_out"], p["w_o"], p["ffn_norm_g"], seq)
    y = _ffn(h2, x1, p["w_up"], p["conv_w"], p["conv_b"], p["w_down"], seq)
    return y.reshape(batch, seq, d)


def kernel(x_prompt, x_sample, mix_norm_g, w_in, b_gate, q_norm_g, k_norm_g, w_pool, pool_scale,
           w_pool_out, w_attn_out, w_o, ffn_norm_g, w_up, conv_w, conv_b, w_down):
    depth = w_in.shape[0]
    pool_w = w_pool_out.shape[1]
    layers = []
    for l in range(depth):
        layers.append(dict(
            pool_w=pool_w,
            mix_norm_g=mix_norm_g[l][None, :], w_in=w_in[l].astype(BF16), b_gate=b_gate[l][None, :],
            q_norm_g=q_norm_g[l][None, :], k_norm_g=k_norm_g[l][None, :],
            w_pool=w_pool[l].astype(BF16), pool_scale=pool_scale[l][None, :],
            w_pool_out=w_pool_out[l].astype(BF16), w_attn_out=w_attn_out[l].astype(BF16),
            w_o=w_o[l].astype(BF16), ffn_norm_g=ffn_norm_g[l][None, :],
            w_up=w_up[l].astype(BF16), conv_w=conv_w[l], conv_b=conv_b[l][None, :],
            w_down=w_down[l].astype(BF16)))

    def trunk(x):
        for p in layers:
            x = _layer(x, p)
        return x

    return (trunk(x_prompt), trunk(x_sample))
```

```python
import functools

import jax
import jax.numpy as jnp
from jax import lax
from jax.experimental import pallas as pl
from jax.experimental.pallas import tpu as pltpu

F32 = jnp.float32
BF16 = jnp.bfloat16

POOL_WINDOWS = (2, 4, 8, 16)
HEAD_DIM = 128
ATTN_GROUPS = ((128, 1), (512, 4), (2048, 16))
HEADS_PER_GROUP = 4
GROUP_WIDTH = HEADS_PER_GROUP * HEAD_DIM
ATTN_WIDTH = len(ATTN_GROUPS) * GROUP_WIDTH
ROPE_THETA = 10000.0
CONV_WIDTH = 3
EPS = 1e-6
NEG_INF = -1e30

LANES = 128
BF16_SUBLANES = 16
V7X_VMEM_BYTES = 64 * 1024 * 1024
VMEM_LIMIT_BYTES = V7X_VMEM_BYTES - 8 * 1024 * 1024

COL_TILE = GROUP_WIDTH
HALO = BF16_SUBLANES


def _tile(n, target):
    t = min(n, target)
    while n % t or t % BF16_SUBLANES:
        t -= 1
    return t


def _const_spec(shape):
    nd = len(shape)
    return pl.BlockSpec(shape, lambda *_: (0,) * nd, pipeline_mode=pl.Buffered(1))


def _params(semantics):
    return pltpu.CompilerParams(dimension_semantics=semantics, vmem_limit_bytes=VMEM_LIMIT_BYTES)


def _in_proj_kernel(x_ref, ng_ref, w_ref, qg_ref, kg_ref, cos_ref, sin_ref,
                    zr_ref, qkv0_ref, qkv1_ref, qkv2_ref, h_ref, acc0_ref, acc1_ref, de_ref,
                    *, q_lo, n_tiles):
    s = pl.program_id(1)
    tm = x_ref.shape[0]
    n_grp = len(ATTN_GROUPS)
    accs = (acc0_ref, acc1_ref)

    def project(tile):
        res = jnp.dot(h_ref[...], w_ref[...], preferred_element_type=F32)
        for h in range(HEADS_PER_GROUP):
            accs[tile % 2][h] = res[:, h * HEAD_DIM:(h + 1) * HEAD_DIM]

    @pl.when(s == 0)
    def _():
        x = x_ref[...]
        inv = lax.rsqrt(jnp.mean(x * x, axis=-1, keepdims=True) + EPS)
        h_ref[...] = (x * inv * ng_ref[...]).astype(BF16)
        project(0)

    for parity in range(2):
        @pl.when((s >= 1) & (s <= q_lo) & ((s - 1) % 2 == parity))
        def _(parity=parity):
            project(parity + 1)
            for h in range(HEADS_PER_GROUP):
                zr_ref[:, h * HEAD_DIM:(h + 1) * HEAD_DIM] = accs[parity][h].astype(BF16)

    def normed_rope(a, gain):
        inv = lax.rsqrt(jnp.mean(a * a, axis=-1, keepdims=True) + EPS)
        a = a * inv * gain
        rot = pltpu.roll(a, HEAD_DIM // 2, axis=1)
        return a * cos_ref[...] + rot * sin_ref[...]

    def finish(which, out_ref, dilation, tile):
        acc_ref = accs[tile % 2]
        for h in range(HEADS_PER_GROUP):
            cols = slice(h * HEAD_DIM, (h + 1) * HEAD_DIM)
            src = acc_ref
            if which < 2:
                a = normed_rope(acc_ref[h], qg_ref[...] if which == 0 else kg_ref[...])
                if dilation == 1:
                    out_ref[which, 0, :, cols] = a.astype(BF16)
                    continue
                de_ref[h] = a
                src = de_ref
            for r in range(dilation):
                rows = src[h, pl.ds(r, tm // dilation, stride=dilation), :]
                out_ref[which, r, :, cols] = rows.astype(BF16)

    qkv_tiles = [(which, out_ref, dilation, q_lo + which * n_grp + g)
                 for which in range(3)
                 for g, (out_ref, (_, dilation)) in enumerate(zip((qkv0_ref, qkv1_ref, qkv2_ref), ATTN_GROUPS))]
    assert qkv_tiles[-1][3] == n_tiles - 1
    for idx, spec in enumerate(qkv_tiles[:-1]):
        @pl.when(s == spec[3] + 1)
        def _(spec=spec, idx=idx):
            project(spec[3] + 1)
            finish(*spec)
            if idx == len(qkv_tiles) - 2:
                finish(*qkv_tiles[-1])


def _in_proj(x2, norm_g, w_in_b, q_norm_g, k_norm_g, cos2, sin2, batch, seq, q_lo):
    t, d = x2.shape
    width = w_in_b.shape[1]
    tm = _tile(seq, 1024)
    pos_tiles = seq // tm
    n_tiles = width // COL_TILE
    n_gate = (2 * d) // COL_TILE
    grid = (t // tm, n_tiles)
    qkv_shapes, qkv_specs = [], []
    for _, dilation in ATTN_GROUPS:
        assert tm % (dilation * BF16_SUBLANES) == 0
        qkv_shapes.append(jax.ShapeDtypeStruct((3, batch, dilation, seq // dilation, GROUP_WIDTH), BF16))
        qkv_specs.append(pl.BlockSpec((3, None, dilation, tm // dilation, GROUP_WIDTH),
                                      lambda i, j: (0, i // pos_tiles, 0, i % pos_tiles, 0)))
    return pl.pallas_call(
        functools.partial(_in_proj_kernel, q_lo=q_lo, n_tiles=n_tiles),
        out_shape=[jax.ShapeDtypeStruct((t, q_lo * COL_TILE), BF16)] + qkv_shapes,
        grid=grid,
        in_specs=[
            pl.BlockSpec((tm, d), lambda i, j: (i, 0)),
            pl.BlockSpec((1, d), lambda i, j: (0, 0)),
            pl.BlockSpec((d, COL_TILE),
                         lambda i, j: (0, (jnp.minimum(j, n_tiles - 1) + n_tiles - n_gate) % n_tiles)),
            pl.BlockSpec((1, HEAD_DIM), lambda i, j: (0, 0)),
            pl.BlockSpec((1, HEAD_DIM), lambda i, j: (0, 0)),
            pl.BlockSpec((tm, HEAD_DIM), lambda i, j: (i % pos_tiles, 0)),
            pl.BlockSpec((tm, HEAD_DIM), lambda i, j: (i % pos_tiles, 0)),
        ],
        out_specs=[pl.BlockSpec((tm, COL_TILE), lambda i, j: (i, jnp.clip(j - 1, 0, q_lo - 1)))] + qkv_specs,
        scratch_shapes=[pltpu.VMEM((tm, d), BF16)] + [pltpu.VMEM((HEADS_PER_GROUP, tm, HEAD_DIM), F32)] * 3,
        compiler_params=_params(("parallel", "arbitrary")),
        name="in_proj",
    )(x2, norm_g, w_in_b, q_norm_g, k_norm_g, cos2, sin2)


def _attn_kernel(q_ref, kp_ref, km_ref, kn_ref, vp_ref, vm_ref, vn_ref, o_ref, lse_ref,
                 kx_ref, vx_ref, *, tq, sub, radius, length):
    i = pl.program_id(2)
    kx_ref[0:radius, :] = kp_ref[...]
    kx_ref[radius:radius + tq, :] = km_ref[...]
    kx_ref[radius + tq:, :] = kn_ref[...]
    vx_ref[0:radius, :] = vp_ref[...]
    vx_ref[radius:radius + tq, :] = vm_ref[...]
    vx_ref[radius + tq:, :] = vn_ref[...]
    nk = sub + 2 * radius
    scale = HEAD_DIM ** -0.5
    rel = lax.broadcasted_iota(jnp.int32, (sub, nk), 1) - radius - lax.broadcasted_iota(jnp.int32, (sub, nk), 0)
    band = jnp.abs(rel) <= radius
    for c in range(tq // sub):
        kpos = i * tq + c * sub - radius + lax.broadcasted_iota(jnp.int32, (sub, nk), 1)
        valid = band & (kpos >= 0) & (kpos < length)
        for h in range(HEADS_PER_GROUP):
            cols = slice(h * HEAD_DIM, (h + 1) * HEAD_DIM)
            q = q_ref[c * sub:(c + 1) * sub, cols]
            k = kx_ref[c * sub:c * sub + nk, cols]
            v = vx_ref[c * sub:c * sub + nk, cols]
            s = lax.dot_general(q, k, (((1,), (1,)), ((), ())), preferred_element_type=F32) * scale
            s = jnp.where(valid, s, NEG_INF)
            m = jnp.max(s, axis=-1, keepdims=True)
            p = jnp.exp(s - m)
            denom = jnp.sum(p, axis=-1, keepdims=True)
            o = jnp.dot(p.astype(BF16), v, preferred_element_type=F32) / denom
            o_ref[c * sub:(c + 1) * sub, cols] = o.astype(o_ref.dtype)
            lse = m + jnp.log(denom)
            lse_ref[c * sub:(c + 1) * sub, cols] = jnp.broadcast_to(lse, (sub, HEAD_DIM))


def _attention_group(qkv, group, window, dilation):
    _, batch, _, length, _ = qkv.shape
    radius = window // (2 * dilation)
    tq = _tile(length, 512)
    sub = min(tq, 2 * radius)
    assert tq % sub == 0 and tq % radius == 0 and length % radius == 0
    halo_per_tile = tq // radius
    n_halo = length // radius

    def main(which):
        return pl.BlockSpec((None, None, None, tq, GROUP_WIDTH), lambda b, r, i: (which, b, r, i, 0))

    def prev(which):
        return pl.BlockSpec((None, None, None, radius, GROUP_WIDTH),
                            lambda b, r, i: (which, b, r, jnp.maximum(i * halo_per_tile - 1, 0), 0))

    def nxt(which):
        return pl.BlockSpec((None, None, None, radius, GROUP_WIDTH),
                            lambda b, r, i: (which, b, r, jnp.minimum((i + 1) * halo_per_tile, n_halo - 1), 0))

    out_spec = pl.BlockSpec((None, None, tq, GROUP_WIDTH), lambda b, r, i: (b, r, i, 0))
    out_shape = (batch, dilation, length, GROUP_WIDTH)
    return pl.pallas_call(
        functools.partial(_attn_kernel, tq=tq, sub=sub, radius=radius, length=length),
        out_shape=(jax.ShapeDtypeStruct(out_shape, BF16), jax.ShapeDtypeStruct(out_shape, F32)),
        grid=(batch, dilation, length // tq),
        in_specs=[main(0), prev(1), main(1), nxt(1), prev(2), main(2), nxt(2)],
        out_specs=(out_spec, out_spec),
        scratch_shapes=[pltpu.VMEM((tq + 2 * radius, GROUP_WIDTH), BF16),
                        pltpu.VMEM((tq + 2 * radius, GROUP_WIDTH), BF16)],
        compiler_params=_params(("parallel", "parallel", "arbitrary")),
        name=f"attn_g{group}",
    )(qkv, qkv, qkv, qkv, qkv, qkv, qkv)


def _merge_kernel(x_ref, g_ref, up_ref, um_ref, un_ref,
                  o0_ref, o1_ref, o2_ref, l0_ref, l1_ref, l2_ref,
                  bg_ref, wp_ref, ps_ref, wpo_ref, wao_ref, wo_ref, fg_ref,
                  x1_ref, h2_ref, ux_ref, oi_ref, li_ref, *, tm, seq):
    i = pl.program_id(0)
    d = x_ref.shape[1]
    pool_w = um_ref.shape[1]
    n_pool = len(POOL_WINDOWS)
    gdim = pool_w // n_pool

    pos0 = (i * tm) % seq
    pos_p = pos0 - HALO + lax.broadcasted_iota(jnp.int32, (HALO, 1), 0)
    pos_n = pos0 + tm + lax.broadcasted_iota(jnp.int32, (HALO, 1), 0)
    ux_ref[0:HALO, :] = jnp.where(pos_p >= 0, up_ref[...].astype(F32), 0.0)
    ux_ref[HALO:HALO + tm, :] = um_ref[...].astype(F32)
    ux_ref[HALO + tm:, :] = jnp.where(pos_n < seq, un_ref[...].astype(F32), 0.0)

    pos = pos0 + lax.broadcasted_iota(jnp.int32, (tm, 1), 0)
    mixed = []
    for g, window in enumerate(POOL_WINDOWS):
        rad = window // 2
        cols = slice(g * gdim, (g + 1) * gdim)
        u = ux_ref[HALO:HALO + tm, cols]
        tot = u
        for t in range(1, rad + 1):
            tot = tot + ux_ref[HALO - t:HALO - t + tm, cols] + ux_ref[HALO + t:HALO + t + tm, cols]
        count = (jnp.minimum(pos + rad + 1, seq) - jnp.maximum(pos - rad, 0)).astype(F32)
        centred = tot / count - u
        mg = jnp.dot(centred.astype(BF16), wp_ref[g], preferred_element_type=F32)
        mixed.append((mg * ps_ref[:, cols]).astype(BF16))
    pool_d = jnp.dot(jnp.concatenate(mixed, axis=1), wpo_ref[...], preferred_element_type=F32)

    outs, lses = [], []
    for g, (o_ref, l_ref) in enumerate(((o0_ref, l0_ref), (o1_ref, l1_ref), (o2_ref, l2_ref))):
        dilation = o_ref.shape[0]
        if dilation == 1:
            outs.append(o_ref[0].astype(F32))
            lses.append(l_ref[0])
        else:
            for h in range(HEADS_PER_GROUP):
                cols = slice(h * HEAD_DIM, (h + 1) * HEAD_DIM)
                for r in range(dilation):
                    rows = pl.ds(r, tm // dilation, stride=dilation)
                    oi_ref[g, h, rows, :] = o_ref[r, :, cols].astype(F32)
                    li_ref[g, h, rows, :] = l_ref[r, :, cols]
            outs.append(jnp.concatenate([oi_ref[g, h] for h in range(HEADS_PER_GROUP)], axis=1))
            lses.append(jnp.concatenate([li_ref[g, h] for h in range(HEADS_PER_GROUP)], axis=1))
    l0, l1, l2 = lses
    lmax = jnp.maximum(jnp.maximum(l0, l1), l2)
    e0, e1, e2 = jnp.exp(l0 - lmax), jnp.exp(l1 - lmax), jnp.exp(l2 - lmax)
    attn = (e0 * outs[0] + e1 * outs[1] + e2 * outs[2]) / (e0 + e1 + e2)
    attn_d = jnp.dot(attn.astype(BF16), wao_ref[...], preferred_element_type=F32)

    gates = jax.nn.sigmoid(g_ref[...].astype(F32) + bg_ref[...])
    merged = gates[:, :d] * pool_d + gates[:, d:] * attn_d
    x1 = x_ref[...] + jnp.dot(merged.astype(BF16), wo_ref[...], preferred_element_type=F32)
    x1_ref[...] = x1
    inv = lax.rsqrt(jnp.mean(x1 * x1, axis=-1, keepdims=True) + EPS)
    h2_ref[...] = (x1 * inv * fg_ref[...]).astype(BF16)


def _merge(x2, z, attn_outs, b_gate, w_pool_b, pool_scale, w_pool_out_b, w_attn_out_b, w_o_b,
           ffn_norm_g, seq):
    t, d = x2.shape
    pool_w = w_pool_out_b.shape[0]
    tm = _tile(seq, 256)
    halo_per_tile = tm // HALO
    n_halo = t // HALO
    u_col = (2 * d) // pool_w

    row = lambda i: (i, 0)
    pos_tiles = seq // tm
    (o0, l0), (o1, l1), (o2, l2) = attn_outs
    grp = []
    for _, dilation in ATTN_GROUPS:
        assert tm % (dilation * BF16_SUBLANES) == 0
        grp.append(pl.BlockSpec((None, dilation, tm // dilation, GROUP_WIDTH),
                                lambda i: (i // pos_tiles, 0, i % pos_tiles, 0)))
    return pl.pallas_call(
        functools.partial(_merge_kernel, tm=tm, seq=seq),
        out_shape=(jax.ShapeDtypeStruct((t, d), F32), jax.ShapeDtypeStruct((t, d), BF16)),
        grid=(t // tm,),
        in_specs=[
            pl.BlockSpec((tm, d), row),
            pl.BlockSpec((tm, 2 * d), row),
            pl.BlockSpec((HALO, pool_w), lambda i: (jnp.maximum(i * halo_per_tile - 1, 0), u_col)),
            pl.BlockSpec((tm, pool_w), lambda i: (i, u_col)),
            pl.BlockSpec((HALO, pool_w), lambda i: (jnp.minimum((i + 1) * halo_per_tile, n_halo - 1), u_col)),
            grp[0], grp[1], grp[2], grp[0], grp[1], grp[2],
            _const_spec(b_gate.shape),
            _const_spec(w_pool_b.shape),
            _const_spec(pool_scale.shape),
            _const_spec(w_pool_out_b.shape),
            _const_spec(w_attn_out_b.shape),
            _const_spec(w_o_b.shape),
            _const_spec(ffn_norm_g.shape),
        ],
        out_specs=(pl.BlockSpec((tm, d), row), pl.BlockSpec((tm, d), row)),
        scratch_shapes=[pltpu.VMEM((tm + 2 * HALO, pool_w), F32),
                        pltpu.VMEM((len(ATTN_GROUPS), HEADS_PER_GROUP, tm, HEAD_DIM), F32),
                        pltpu.VMEM((len(ATTN_GROUPS), HEADS_PER_GROUP, tm, HEAD_DIM), F32)],
        compiler_params=_params(("parallel",)),
        name="merge",
    )(x2, z, z, z, z, o0, o1, o2, l0, l1, l2,
      b_gate, w_pool_b, pool_scale, w_pool_out_b, w_attn_out_b, w_o_b, ffn_norm_g)


def _ffn_kernel(hp_ref, hm_ref, hn_ref, x1_ref, wg_ref, wv_ref, cw_ref, cb_ref, wd_ref,
                y_ref, hx_ref, *, tm, seq):
    i = pl.program_id(0)
    f = pl.program_id(1)

    @pl.when(f == 0)
    def _():
        pos0 = (i * tm) % seq
        pos_p = pos0 - HALO + lax.broadcasted_iota(jnp.int32, (HALO, 1), 0)
        pos_n = pos0 + tm + lax.broadcasted_iota(jnp.int32, (HALO, 1), 0)
        hx_ref[0:HALO, :] = jnp.where(pos_p >= 0, hp_ref[...], jnp.zeros_like(hp_ref[...]))
        hx_ref[HALO:HALO + tm, :] = hm_ref[...]
        hx_ref[HALO + tm:, :] = jnp.where(pos_n < seq, hn_ref[...], jnp.zeros_like(hn_ref[...]))
        y_ref[...] = x1_ref[...]

    gate = jnp.dot(hx_ref[...], wg_ref[...], preferred_element_type=F32)
    val = jnp.dot(hm_ref[...], wv_ref[...], preferred_element_type=F32)
    half = CONV_WIDTH // 2
    conv = cb_ref[...]
    for tap in range(CONV_WIDTH):
        lo = HALO - half + tap
        conv = conv + gate[lo:lo + tm, :] * cw_ref[tap:tap + 1, :]
    act = jax.nn.gelu(conv) * val
    y_ref[...] += jnp.dot(act.astype(BF16), wd_ref[...], preferred_element_type=F32)


def _ffn(h2, x1, w_up_b, conv_w, conv_b, w_down_b, seq):
    t, d = x1.shape
    d_ff = w_down_b.shape[0]
    tm = _tile(seq, 512)
    tf = 512 if d_ff % 512 == 0 else 256
    nf = d_ff // tf
    halo_per_tile = tm // HALO
    n_halo = t // HALO
    return pl.pallas_call(
        functools.partial(_ffn_kernel, tm=tm, seq=seq),
        out_shape=jax.ShapeDtypeStruct((t, d), F32),
        grid=(t // tm, nf),
        in_specs=[
            pl.BlockSpec((HALO, d), lambda i, f: (jnp.maximum(i * halo_per_tile - 1, 0), 0)),
            pl.BlockSpec((tm, d), lambda i, f: (i, 0)),
            pl.BlockSpec((HALO, d), lambda i, f: (jnp.minimum((i + 1) * halo_per_tile, n_halo - 1), 0)),
            pl.BlockSpec((tm, d), lambda i, f: (i, 0)),
            pl.BlockSpec((d, tf), lambda i, f: (0, f)),
            pl.BlockSpec((d, tf), lambda i, f: (0, nf + f)),
            pl.BlockSpec((CONV_WIDTH, tf), lambda i, f: (0, f)),
            pl.BlockSpec((1, tf), lambda i, f: (0, f)),
            pl.BlockSpec((tf, d), lambda i, f: (f, 0)),
        ],
        out_specs=pl.BlockSpec((tm, d), lambda i, f: (i, 0)),
        scratch_shapes=[pltpu.VMEM((tm + 2 * HALO, d), BF16)],
        compiler_params=_params(("parallel", "arbitrary")),
        name="ffn",
    )(h2, h2, h2, x1, w_up_b, w_up_b, conv_w, conv_b, w_down_b)


def _rope_tables(seq):
    half = HEAD_DIM // 2
    inv_freq = ROPE_THETA ** (-jnp.arange(half, dtype=F32) / half)
    ang = jnp.arange(seq, dtype=F32)[:, None] * inv_freq[None, :]
    cos, sin = jnp.cos(ang), jnp.sin(ang)
    return jnp.concatenate([cos, cos], axis=-1), jnp.concatenate([-sin, sin], axis=-1)


def _layer(x, p):
    batch, seq, d = x.shape
    x2 = x.reshape(batch * seq, d)
    cos2, sin2 = _rope_tables(seq)
    q_lo = (2 * d + p["pool_w"]) // COL_TILE
    z, *qkv = _in_proj(x2, p["mix_norm_g"], p["w_in"], p["q_norm_g"], p["k_norm_g"], cos2, sin2,
                       batch, seq, q_lo)
    attn_outs = [_attention_group(qkv[g], g, window, dilation)
                 for g, (window, dilation) in enumerate(ATTN_GROUPS)]
    x1, h2 = _merge(x2, z, attn_outs, p["b_gate"], p["w_pool"], p["pool_scale"], p["w_pool_out"],
                    p["w_attn_out"], p["w_o"], p["ffn_norm_g"], seq)
    y = _ffn(h2, x1, p["w_up"], p["conv_w"], p["conv_b"], p["w_down"], seq)
    return y.reshape(batch, seq, d)


def kernel(x_prompt, x_sample, mix_norm_g, w_in, b_gate, q_norm_g, k_norm_g, w_pool, pool_scale,
           w_pool_out, w_attn_out, w_o, ffn_norm_g, w_up, conv_w, conv_b, w_down):
    depth = w_in.shape[0]
    pool_w = w_pool_out.shape[1]
    layers = []
    for l in range(depth):
        layers.append(dict(
            pool_w=pool_w,
            mix_norm_g=mix_norm_g[l][None, :], w_in=w_in[l].astype(BF16), b_gate=b_gate[l][None, :],
            q_norm_g=q_norm_g[l][None, :], k_norm_g=k_norm_g[l][None, :],
            w_pool=w_pool[l].astype(BF16), pool_scale=pool_scale[l][None, :],
            w_pool_out=w_pool_out[l].astype(BF16), w_attn_out=w_attn_out[l].astype(BF16),
            w_o=w_o[l].astype(BF16), ffn_norm_g=ffn_norm_g[l][None, :],
            w_up=w_up[l].astype(BF16), conv_w=conv_w[l], conv_b=conv_b[l][None, :],
            w_down=w_down[l].astype(BF16)))

    def trunk(x):
        for p in layers:
            x = _layer(x, p)
        return x

    return (trunk(x_prompt), trunk(x_sample))
```
